```python
import math
import jax, jax.numpy as jnp
from jax import lax
import numpy as np

D_MODEL = 2048
BATCH = 4
SEQ = 4096
DEPTH = 2

CTX_LEN = 256
GRID_W = 64
N_EVEN = (DEPTH + 1) // 2
N_ODD = DEPTH // 2
N_MOD = 9
D_FF = 5632
EPS = 1e-6

SSM_HEADS = 32
SSM_HEAD_DIM = 64
SSM_INNER = SSM_HEADS * SSM_HEAD_DIM
SSM_GROUPS = 4
SSM_STATE = 128
SSM_CONV = 5
SSM_CHUNK = 128
SSM_GN = SSM_GROUPS * SSM_STATE
SSM_CONV_CH = SSM_INNER + 2 * SSM_GN

MLA_HEADS = 16
Q_LORA = 512
KV_LORA = 512
QK_NOPE = 128
QK_ROPE = 64
V_HEAD = 128
MLA_OUT = MLA_HEADS * V_HEAD
MLA_SCALE = (QK_NOPE + QK_ROPE) ** -0.5
ROPE_THETA = 10000.0
ROPE_HALF = QK_ROPE // 2
ROPE_AXIS_FREQS = QK_ROPE // 4
Q_BLOCK = 128

IN_SIZES = (Q_LORA, KV_LORA, QK_ROPE, SSM_INNER, SSM_CONV_CH, 2 * SSM_HEADS)
IN_OFFSETS = tuple(int(v) for v in np.cumsum(IN_SIZES)[:-1])
IN_TOTAL = sum(IN_SIZES)
MIX_WIDTH = SSM_INNER + MLA_OUT

POOL_WINDOWS = (2, 4, 8, 16)
POOL_GROUP = D_MODEL // len(POOL_WINDOWS)

kernel_name = 'hybrid_ssd_mla_pool_macaron_dit'


def rmsnorm(u, gain=None):
    uf = u.astype(jnp.float32)
    y = uf * lax.rsqrt(jnp.mean(uf * uf, axis=-1, keepdims=True) + EPS)
    if gain is not None:
        y = y * gain
    return y.astype(u.dtype)


def modulate(u, shift, scale):
    return u * (1 + scale) + shift


def adaln(cond, w, b):
    return jnp.split(jax.nn.silu(cond) @ w + b, N_MOD, axis=-1)


def swiglu(u, wg, wu, wd):
    return (jax.nn.silu(u @ wg) * (u @ wu)) @ wd


def axial_rope(rows):
    f32 = jnp.float32
    row = jnp.repeat(jnp.arange(rows, dtype=f32), GRID_W)
    col = jnp.tile(jnp.arange(GRID_W, dtype=f32), rows)
    inv = ROPE_THETA ** (-jnp.arange(ROPE_AXIS_FREQS, dtype=f32) / ROPE_AXIS_FREQS)
    ang = jnp.concatenate([row[:, None] * inv, col[:, None] * inv], axis=-1)
    return jnp.cos(ang), jnp.sin(ang)


def apply_rope(u, cos, sin):
    u1, u2 = u[..., :ROPE_HALF], u[..., ROPE_HALF:]
    return jnp.concatenate([u1 * cos - u2 * sin, u1 * sin + u2 * cos], axis=-1).astype(u.dtype)


def dconv(u, w, bias):
    y = lax.conv_general_dilated(u, w[:, None, :], window_strides=(1,),
                                 padding=((SSM_CONV // 2, SSM_CONV // 2),),
                                 dimension_numbers=('NWC', 'WIO', 'NWC'),
                                 feature_group_count=u.shape[-1])
    return y + bias


def segsum(a):
    t = a.shape[-1]
    ar = jnp.broadcast_to(a[..., :, None], a.shape + (t,))
    ar = jnp.where(jnp.tril(jnp.ones((t, t), bool), -1), ar, 0.0)
    s = jnp.cumsum(ar, axis=-2)
    return jnp.where(jnp.tril(jnp.ones((t, t), bool)), s, -jnp.inf)


def ssd(x, dt, a_neg, bm, cm, init):
    bsz, n, nh, hp = x.shape
    ng, ns = bm.shape[2], bm.shape[3]
    nr = nh // ng
    nc, cl = n // SSM_CHUNK, SSM_CHUNK
    f32 = jnp.float32
    xd = (x.astype(f32) * dt[..., None]).reshape(bsz, nc, cl, ng, nr, hp)
    a = (dt * a_neg).reshape(bsz, nc, cl, ng, nr).transpose(0, 3, 4, 1, 2)
    bc = bm.astype(f32).reshape(bsz, nc, cl, ng, ns)
    cc = cm.astype(f32).reshape(bsz, nc, cl, ng, ns)
    a_cs = jnp.cumsum(a, axis=-1)
    cb = jnp.einsum('bclgn,bcsgn->bcgls', cc, bc)
    y_diag = jnp.einsum('bcgls,bgrcls,bcsgrp->bclgrp', cb, jnp.exp(segsum(a)), xd)
    decay_states = jnp.exp(a_cs[..., -1:] - a_cs)
    states = jnp.einsum('bclgn,bgrcl,bclgrp->bcgrpn', bc, decay_states, xd)
    states = jnp.concatenate([init.astype(f32).reshape(bsz, 1, ng, nr, hp, ns), states], axis=1)
    decay_chunk = jnp.exp(segsum(jnp.pad(a_cs[..., -1], ((0, 0), (0, 0), (0, 0), (1, 0)))))
    new_states = jnp.einsum('bgrzc,bcgrpn->bzgrpn', decay_chunk, states)
    y_off = jnp.einsum('bclgn,bcgrpn,bgrcl->bclgrp', cc, new_states[:, :-1], jnp.exp(a_cs))
    y = (y_diag + y_off).reshape(bsz, n, nh, hp).astype(x.dtype)
    return y, new_states[:, -1].reshape(bsz, nh, hp, ns)


def gated_group_rmsnorm(y, z, gain):
    g = (y * jax.nn.silu(z)).astype(jnp.float32)
    gg = g.reshape(g.shape[:-1] + (SSM_GROUPS, SSM_INNER // SSM_GROUPS))
    gg = gg * lax.rsqrt(jnp.mean(gg * gg, axis=-1, keepdims=True) + EPS)
    return (gg.reshape(g.shape) * gain).astype(y.dtype)


def mla_attend(q_nope, q_rope, k_nope, k_rope, v):
    bsz, n, nh, _ = q_nope.shape
    nb = n // Q_BLOCK
    qn = jnp.moveaxis(q_nope.reshape(bsz, nb, Q_BLOCK, nh, QK_NOPE), 1, 0)
    qr = jnp.moveaxis(q_rope.reshape(bsz, nb, Q_BLOCK, nh, QK_ROPE), 1, 0)

    def block(qs):
        qn_b, qr_b = qs
        s = jnp.einsum('bqhd,bkhd->bhqk', qn_b, k_nope) + jnp.einsum('bqhd,bkd->bhqk', qr_b, k_rope)
        pr = jax.nn.softmax(s.astype(jnp.float32) * MLA_SCALE, axis=-1).astype(v.dtype)
        return jnp.einsum('bhqk,bkhd->bqhd', pr, v)

    o = lax.map(block, (qn, qr))
    return jnp.moveaxis(o, 0, 1).reshape(bsz, n, nh * V_HEAD)


def ssd_mla_mixer(h, hc, ctx_out, cos, sin, w_in, conv_w, conv_b, dt_bias, a_log, d_skip,
                  ssm_norm, q_norm, kv_norm, w_uq, w_ukv, w_out):
    bsz, n, _ = h.shape
    a_neg = -jnp.exp(a_log.astype(jnp.float32))
    flip = lambda t: jnp.flip(t, axis=1)

    def mla_q(qc, rot):
        q = (rmsnorm(qc, q_norm) @ w_uq).reshape(bsz, qc.shape[1], MLA_HEADS, QK_NOPE + QK_ROPE)
        qr = q[..., QK_NOPE:]
        if rot:
            qr = apply_rope(qr, cos[:, None, :], sin[:, None, :])
        return q[..., :QK_NOPE], qr

    def mla_kv(kvc, kr, rot):
        kv = (rmsnorm(kvc, kv_norm) @ w_ukv).reshape(bsz, kvc.shape[1], MLA_HEADS, QK_NOPE + V_HEAD)
        if rot:
            kr = apply_rope(kr, cos, sin)
        return kv[..., :QK_NOPE], kr, kv[..., QK_NOPE:]

    def ssm_prep(xbc, dtr):
        m = xbc.shape[1]
        u = jax.nn.silu(dconv(xbc, conv_w, conv_b))
        xs = u[..., :SSM_INNER].reshape(bsz, m, SSM_HEADS, SSM_HEAD_DIM)
        bm = u[..., SSM_INNER:SSM_INNER + SSM_GN].reshape(bsz, m, SSM_GROUPS, SSM_STATE)
        cm = u[..., SSM_INNER + SSM_GN:].reshape(bsz, m, SSM_GROUPS, SSM_STATE)
        dt = jax.nn.softplus(dtr.astype(jnp.float32).reshape(bsz, m, 2, SSM_HEADS)
                             + dt_bias.astype(jnp.float32))
        return xs, bm, cm, dt[:, :, 0], dt[:, :, 1]

    def bidir(xs, bm, cm, dtf, dtb, init_f, init_b):
        yf, sf = ssd(xs, dtf, a_neg[0], bm, cm, init_f)
        yb, sb = ssd(flip(xs), flip(dtb), a_neg[1], flip(bm), flip(cm), init_b)
        return yf + flip(yb) + d_skip[:, None] * xs, sf, sb

    q_c, kv_c, k_r, z, xbc, dtr = jnp.split(h @ w_in, IN_OFFSETS, axis=-1)
    q_cc, kv_cc, k_rc, z_c, xbc_c, dtr_c = jnp.split(hc @ w_in, IN_OFFSETS, axis=-1)

    zeros = jnp.zeros((bsz, SSM_HEADS, SSM_HEAD_DIM, SSM_STATE), jnp.float32)
    y_c, sf_c, sb_c = bidir(*ssm_prep(xbc_c, dtr_c), zeros, zeros)
    kn_c, kr_c, v_c = mla_kv(kv_cc, k_rc, False)

    y_l, _, _ = bidir(*ssm_prep(xbc, dtr), sf_c, sb_c)
    kn_l, kr_l, v_l = mla_kv(kv_c, k_r, True)
    qn, qr = mla_q(q_c, True)
    o = mla_attend(qn, qr, jnp.concatenate([kn_c, kn_l], axis=1),
                   jnp.concatenate([kr_c, kr_l], axis=1), jnp.concatenate([v_c, v_l], axis=1))
    y_ssm = gated_group_rmsnorm(y_l.reshape(bsz, n, SSM_INNER), z, ssm_norm)
    out = jnp.concatenate([y_ssm, o], axis=-1) @ w_out
    if not ctx_out:
        return out, None
    qn_c, qr_c = mla_q(q_cc, False)
    o_c = mla_attend(qn_c, qr_c, kn_c, kr_c, v_c)
    y_ssm_c = gated_group_rmsnorm(y_c.reshape(bsz, hc.shape[1], SSM_INNER), z_c, ssm_norm)
    out_c = jnp.concatenate([y_ssm_c, o_c], axis=-1) @ w_out
    return out, out_c


def pool_mixer(h, w_pool, scale):
    n = h.shape[1]
    hf = h.astype(jnp.float32)
    cs = jnp.pad(jnp.cumsum(hf, axis=1), ((0, 0), (1, 0), (0, 0)))
    t = jnp.arange(n)
    groups = []
    for gi, w in enumerate(POOL_WINDOWS):
        lo = jnp.clip(t - w // 2, 0, n)
        hi = jnp.clip(t + w // 2, 0, n)
        sl = slice(gi * POOL_GROUP, (gi + 1) * POOL_GROUP)
        csg = cs[:, :, sl]
        mean = (csg[:, hi] - csg[:, lo]) / (hi - lo).astype(jnp.float32)[:, None]
        groups.append(mean - hf[:, :, sl])
    pooled = jnp.stack(groups, axis=2).astype(h.dtype)
    out = jnp.einsum('bngi,gio->bngo', pooled, w_pool).reshape(h.shape)
    return out * scale


def setup_inputs(seed: int = 0) -> dict:
    key = jax.random.key(seed)
    ks = iter(jax.random.split(key, 32))
    f32 = jnp.float32

    def nrm(shape, scale):
        return jax.random.normal(next(ks), shape, f32) * scale

    def gain(shape):
        return 1.0 + nrm(shape, 0.05)

    dt0 = jnp.exp(jax.random.uniform(next(ks), (N_EVEN, 2, SSM_HEADS), f32, math.log(1e-3), math.log(1e-1)))
    a0 = jax.random.uniform(next(ks), (N_EVEN, 2, SSM_HEADS), f32, 1.0, 16.0)
    return {
        'x': nrm((BATCH, SEQ, D_MODEL), 1.0),
        'c': nrm((BATCH, D_MODEL), 1.0),
        'ctx': nrm((BATCH, CTX_LEN, D_MODEL), 1.0),
        'c_ctx': nrm((D_MODEL,), 1.0),
        'mod_w': nrm((DEPTH, D_MODEL, N_MOD * D_MODEL), 0.5 * D_MODEL ** -0.5),
        'mod_b': nrm((DEPTH, N_MOD * D_MODEL), 0.01),
        'ffn_w_gate': nrm((DEPTH, 2, D_MODEL, D_FF), D_MODEL ** -0.5),
        'ffn_w_up': nrm((DEPTH, 2, D_MODEL, D_FF), D_MODEL ** -0.5),
        'ffn_w_down': nrm((DEPTH, 2, D_FF, D_MODEL), D_FF ** -0.5),
        'w_in': nrm((N_EVEN, D_MODEL, IN_TOTAL), D_MODEL ** -0.5),
        'conv_w': nrm((N_EVEN, SSM_CONV, SSM_CONV_CH), SSM_CONV ** -0.5),
        'conv_b': nrm((N_EVEN, SSM_CONV_CH), 0.01),
        'dt_bias': dt0 + jnp.log(-jnp.expm1(-dt0)),
        'a_log': jnp.log(a0),
        'd_skip': gain((N_EVEN, SSM_HEADS)),
        'ssm_norm': gain((N_EVEN, SSM_INNER)),
        'q_norm': gain((N_EVEN, Q_LORA)),
        'kv_norm': gain((N_EVEN, KV_LORA)),
        'w_uq': nrm((N_EVEN, Q_LORA, MLA_HEADS * (QK_NOPE + QK_ROPE)), Q_LORA ** -0.5),
        'w_ukv': nrm((N_EVEN, KV_LORA, MLA_HEADS * (QK_NOPE + V_HEAD)), KV_LORA ** -0.5),
        'w_out': nrm((N_EVEN, MIX_WIDTH, D_MODEL), MIX_WIDTH ** -0.5),
        'pool_w': nrm((N_ODD, len(POOL_WINDOWS), POOL_GROUP, POOL_GROUP), POOL_GROUP ** -0.5),
        'pool_scale': gain((N_ODD, D_MODEL)),
        'final_norm': gain((D_MODEL,)),
    }


def reference(x, c, ctx, c_ctx, mod_w, mod_b, ffn_w_gate, ffn_w_up, ffn_w_down, w_in, conv_w, conv_b,
              dt_bias, a_log, d_skip, ssm_norm, q_norm, kv_norm, w_uq, w_ukv, w_out, pool_w,
              pool_scale, final_norm):
    rows = x.shape[1] // GRID_W
    cos, sin = axial_rope(rows)
    h, hc = x, ctx
    for l in range(DEPTH):
        j = l // 2
        even = l % 2 == 0
        ctx_out = any(k % 2 == 0 for k in range(l + 1, DEPTH))
        ctx_live = even or ctx_out
        m = [t[:, None, :] for t in adaln(c, mod_w[l], mod_b[l])]
        h = h + 0.5 * m[2] * swiglu(modulate(rmsnorm(h), m[0], m[1]),
                                    ffn_w_gate[l, 0], ffn_w_up[l, 0], ffn_w_down[l, 0])
        if ctx_live:
            mc = adaln(c_ctx, mod_w[l], mod_b[l])
            hc = hc + 0.5 * mc[2] * swiglu(modulate(rmsnorm(hc), mc[0], mc[1]),
                                           ffn_w_gate[l, 0], ffn_w_up[l, 0], ffn_w_down[l, 0])
            hcn = modulate(rmsnorm(hc), mc[3], mc[4])
        hn = modulate(rmsnorm(h), m[3], m[4])
        if even:
            mix, mix_c = ssd_mla_mixer(hn, hcn, ctx_out, cos, sin, w_in[j], conv_w[j], conv_b[j],
                                       dt_bias[j], a_log[j], d_skip[j], ssm_norm[j], q_norm[j],
                                       kv_norm[j], w_uq[j], w_ukv[j], w_out[j])
        else:
            mix = pool_mixer(hn, pool_w[j], pool_scale[j])
            mix_c = pool_mixer(hcn, pool_w[j], pool_scale[j]) if ctx_out else None
        h = h + m[5] * mix
        h = h + 0.5 * m[8] * swiglu(modulate(rmsnorm(h), m[6], m[7]),
                                    ffn_w_gate[l, 1], ffn_w_up[l, 1], ffn_w_down[l, 1])
        if ctx_out:
            hc = hc + mc[5] * mix_c
            hc = hc + 0.5 * mc[8] * swiglu(modulate(rmsnorm(hc), mc[6], mc[7]),
                                           ffn_w_gate[l, 1], ffn_w_up[l, 1], ffn_w_down[l, 1])
    return rmsnorm(h, final_norm)
```

```python
import functools

import jax
import jax.numpy as jnp
import numpy as np
from jax import lax
from jax.experimental import pallas as pl
from jax.experimental.pallas import tpu as pltpu

F32 = jnp.float32
BF16 = jnp.bfloat16
HIGHEST = lax.Precision.HIGHEST

EPS = 1e-6
N_MOD = 9
MOD_ROWS = 8
GRID_W = 64

SSM_HEADS = 32
SSM_HEAD_DIM = 64
SSM_INNER = SSM_HEADS * SSM_HEAD_DIM
SSM_GROUPS = 4
SSM_STATE = 128
SSM_CONV = 5
SSM_CHUNK = 128
SSM_GN = SSM_GROUPS * SSM_STATE
SSM_CONV_CH = SSM_INNER + 2 * SSM_GN
HEADS_PER_GROUP = SSM_HEADS // SSM_GROUPS
GROUP_W = HEADS_PER_GROUP * SSM_HEAD_DIM

MLA_HEADS = 16
Q_LORA = 512
KV_LORA = 512
QK_NOPE = 128
QK_ROPE = 64
V_HEAD = 128
MLA_OUT = MLA_HEADS * V_HEAD
MLA_SCALE = (QK_NOPE + QK_ROPE) ** -0.5
ROPE_THETA = 10000.0
ROPE_HALF = QK_ROPE // 2
ROPE_AXIS_FREQS = QK_ROPE // 4
HEAD_PAD = 256

LANE = 128
HALO = 8

P_Z = 0
P_XBC = P_Z + SSM_INNER
P_QC = P_XBC + SSM_CONV_CH
P_KVC = P_QC + Q_LORA
P_KR = P_KVC + KV_LORA
P_COLS = 6400
DT_LANE = QK_ROPE

POOL_WINDOWS = (2, 4, 8, 16)


def _tile(dim, pref):
    t = min(dim, pref)
    while dim % t:
        t //= 2
    return t


def _cparams(sem, vmem_mb):
    return pltpu.CompilerParams(dimension_semantics=sem, vmem_limit_bytes=vmem_mb << 20)


def _rms(x):
    return x * lax.rsqrt(jnp.mean(x * x, axis=-1, keepdims=True) + EPS)


def _mod_spec(d, layer, k, row):
    if row is None:
        return pl.BlockSpec((None, 1, d), lambda b, *_: ((layer * MOD_ROWS + b) * N_MOD + k, 0, 0))
    return pl.BlockSpec((None, 1, d), lambda b, *_: ((layer * MOD_ROWS + row) * N_MOD + k, 0, 0))


def _adaln_kernel(c_ref, w_ref, b_ref, o_ref):
    a = jax.nn.silu(c_ref[...]).astype(BF16)
    o_ref[...] = jnp.dot(a, w_ref[...].astype(BF16), preferred_element_type=F32) + b_ref[...]


def _adaln(cond, mod_w, mod_b):
    depth, d, nd = mod_w.shape
    tn = _tile(nd, 1024)
    return pl.pallas_call(
        _adaln_kernel,
        grid=(depth, nd // tn),
        in_specs=[pl.BlockSpec((MOD_ROWS, d), lambda l, j: (0, 0)),
                  pl.BlockSpec((None, d, tn), lambda l, j: (l, 0, j)),
                  pl.BlockSpec((None, 1, tn), lambda l, j: (l, 0, j))],
        out_specs=pl.BlockSpec((None, MOD_ROWS, tn), lambda l, j: (l, 0, j)),
        out_shape=jax.ShapeDtypeStruct((depth, MOD_ROWS, nd), F32),
        compiler_params=_cparams(("parallel", "parallel"), 40),
        name="adaln",
    )(cond, mod_w, mod_b.reshape(depth, 1, nd))


def _ffn_kernel(*refs, final_norm):
    if final_norm:
        h_ref, sh_ref, sc_ref, gt_ref, wg_ref, wu_ref, wd_ref, fn_ref, o_ref, u_scr, acc_scr = refs
    else:
        h_ref, sh_ref, sc_ref, gt_ref, wg_ref, wu_ref, wd_ref, o_ref, u_scr, acc_scr = refs
    j = pl.program_id(2)

    @pl.when(j == 0)
    def _():
        u = _rms(h_ref[...]) * (1.0 + sc_ref[...]) + sh_ref[...]
        u_scr[...] = u.astype(BF16)
        acc_scr[...] = jnp.zeros_like(acc_scr)

    u = u_scr[...]
    g = jnp.dot(u, wg_ref[...], preferred_element_type=F32)
    up = jnp.dot(u, wu_ref[...], preferred_element_type=F32)
    a = (jax.nn.silu(g) * up).astype(BF16)
    acc_scr[...] += jnp.dot(a, wd_ref[...], preferred_element_type=F32)

    @pl.when(j == pl.num_programs(2) - 1)
    def _():
        out = h_ref[...] + (0.5 * gt_ref[...]) * acc_scr[...]
        if final_norm:
            out = _rms(out) * fn_ref[...]
        o_ref[...] = out


def _ffn(h, mods, layer, half, row, wg, wu, wd, final_gain=None):
    bsz, n, d = h.shape
    f = wg.shape[-1]
    tm, tf = _tile(n, 512), _tile(f, 512)
    k0 = 6 * half
    in_specs = [pl.BlockSpec((None, tm, d), lambda b, i, j: (b, i, 0)),
                _mod_spec(d, layer, k0, row), _mod_spec(d, layer, k0 + 1, row), _mod_spec(d, layer, k0 + 2, row),
                pl.BlockSpec((None, None, d, tf), lambda b, i, j: (layer, half, 0, j)),
                pl.BlockSpec((None, None, d, tf), lambda b, i, j: (layer, half, 0, j)),
                pl.BlockSpec((None, None, tf, d), lambda b, i, j: (layer, half, j, 0))]
    args = [h, mods, mods, mods, wg, wu, wd]
    if final_gain is not None:
        in_specs.append(pl.BlockSpec((1, d), lambda b, i, j: (0, 0)))
        args.append(final_gain.reshape(1, d))
    return pl.pallas_call(
        functools.partial(_ffn_kernel, final_norm=final_gain is not None),
        grid=(bsz, n // tm, f // tf),
        in_specs=in_specs,
        out_specs=pl.BlockSpec((None, tm, d), lambda b, i, j: (b, i, 0)),
        out_shape=jax.ShapeDtypeStruct((bsz, n, d), F32),
        scratch_shapes=[pltpu.VMEM((tm, d), BF16), pltpu.VMEM((tm, d), F32)],
        compiler_params=_cparams(("parallel", "parallel", "arbitrary"), 48),
        name="ffn",
    )(*args)


def _inproj_kernel(*refs, aliased):
    if aliased:
        h_ref, sh_ref, sc_ref, w_ref, _, o_ref, u_scr = refs
    else:
        h_ref, sh_ref, sc_ref, w_ref, o_ref, u_scr = refs

    @pl.when(pl.program_id(2) == 0)
    def _():
        u = _rms(h_ref[...]) * (1.0 + sc_ref[...]) + sh_ref[...]
        u_scr[...] = u.astype(BF16)

    o_ref[...] = jnp.dot(u_scr[...], w_ref[...], preferred_element_type=F32)


def _inproj(h, mods, layer, row, w, n_total, row_off, prev=None):
    bsz, n, d = h.shape
    cols = w.shape[1]
    tm, tn = _tile(n, 512), _tile(cols, 1280)
    off = row_off // tm
    in_specs = [pl.BlockSpec((None, tm, d), lambda b, i, j: (b, i, 0)),
                _mod_spec(d, layer, 3, row), _mod_spec(d, layer, 4, row),
                pl.BlockSpec((d, tn), lambda b, i, j: (0, j))]
    args = [h, mods, mods, w]
    aliases = {}
    if prev is not None:
        in_specs.append(pl.BlockSpec(memory_space=pl.ANY))
        args.append(prev)
        aliases = {4: 0}
    return pl.pallas_call(
        functools.partial(_inproj_kernel, aliased=prev is not None),
        grid=(bsz, n // tm, cols // tn),
        in_specs=in_specs,
        out_specs=pl.BlockSpec((None, tm, tn), lambda b, i, j: (b, off + i, j)),
        out_shape=jax.ShapeDtypeStruct((bsz, n_total, cols), F32),
        scratch_shapes=[pltpu.VMEM((tm, d), BF16)],
        input_output_aliases=aliases,
        compiler_params=_cparams(("parallel", "parallel", "arbitrary"), 40),
        name="inproj",
    )(*args)


def _conv_kernel(cur_ref, prev_ref, next_ref, dtr_ref, w_ref, b_ref, dtb_ref, u_ref, dt_ref, *, lat_tiles):
    t = pl.program_id(1)
    tc = cur_ref.shape[0]
    first = jnp.logical_or(t == 0, t == lat_tiles)
    last = jnp.logical_or(t == lat_tiles - 1, t == pl.num_programs(1) - 1)
    prev = jnp.where(first, 0.0, prev_ref[...])
    nxt = jnp.where(last, 0.0, next_ref[...])
    xe = jnp.concatenate([prev, cur_ref[...], nxt], axis=0)
    w = w_ref[...]
    acc = b_ref[...] + w[0:1, :] * xe[HALO - 2:HALO - 2 + tc, :]
    for k in range(1, SSM_CONV):
        acc = acc + w[k:k + 1, :] * xe[HALO - 2 + k:HALO - 2 + k + tc, :]
    u_ref[...] = jax.nn.silu(acc)

    @pl.when(pl.program_id(2) == 0)
    def _():
        dt_ref[...] = jax.nn.softplus(dtr_ref[...] + dtb_ref[...])


def _conv(p, conv_w8, conv_b, dt_bias_row, n_lat):
    bsz, n_total, _ = p.shape
    tc = 256
    cw = 1024
    assert n_lat % tc == 0 and n_total % tc == 0 and SSM_CONV_CH % cw == 0 and P_XBC % cw == 0
    c0 = P_XBC // cw
    hb = tc // HALO
    last_hb = n_total // HALO - 1
    return pl.pallas_call(
        functools.partial(_conv_kernel, lat_tiles=n_lat // tc),
        grid=(bsz, n_total // tc, SSM_CONV_CH // cw),
        in_specs=[pl.BlockSpec((None, tc, cw), lambda b, t, c: (b, t, c0 + c)),
                  pl.BlockSpec((None, HALO, cw), lambda b, t, c: (b, jnp.maximum(t * hb - 1, 0), c0 + c)),
                  pl.BlockSpec((None, HALO, cw), lambda b, t, c: (b, jnp.minimum((t + 1) * hb, last_hb), c0 + c)),
                  pl.BlockSpec((None, tc, LANE), lambda b, t, c: (b, t, P_KR // LANE)),
                  pl.BlockSpec((HALO, cw), lambda b, t, c: (0, c)),
                  pl.BlockSpec((1, cw), lambda b, t, c: (0, c)),
                  pl.BlockSpec((1, LANE), lambda b, t, c: (0, 0))],
        out_specs=[pl.BlockSpec((None, tc, cw), lambda b, t, c: (b, t, c)),
                   pl.BlockSpec((None, tc, LANE), lambda b, t, c: (b, t, 0))],
        out_shape=[jax.ShapeDtypeStruct((bsz, n_total, SSM_CONV_CH), F32),
                   jax.ShapeDtypeStruct((bsz, n_total, LANE), F32)],
        compiler_params=_cparams(("parallel", "parallel", "arbitrary"), 32),
        name="conv",
    )(p, p, p, p, conv_w8, conv_b, dt_bias_row)


def _ssd_direction(xs_ref, b_ref, c_ref, dt_ref, sel_ref, tri_ref, aneg_ref, exp_ref, st_ref, fwd):
    L = SSM_CHUNK
    hi = functools.partial(jnp.dot, precision=HIGHEST, preferred_element_type=F32)
    dtg = hi(dt_ref[...], sel_ref[...])
    a = dtg * aneg_ref[...]
    dcum = hi(tri_ref[...], a)
    expand = exp_ref[...]
    dte = hi(dtg, expand)
    dce = hi(dcum, expand)
    dtot = dce[L - 1:L, :] if fwd else dce[0:1, :]
    xs = xs_ref[...]
    xd = xs * dte
    cm = c_ref[...].astype(BF16)
    bt = b_ref[...].T.astype(BF16)
    cb = jnp.dot(cm, bt, preferred_element_type=F32)
    st = st_ref[...]
    y = jnp.dot(cm, st.astype(BF16), preferred_element_type=F32) * jnp.exp(dce)
    xdd = (xd * jnp.exp(dtot - dce)).astype(BF16)
    st_ref[...] = jnp.exp(dtot) * st + jnp.dot(bt, xdd, preferred_element_type=F32)
    dct = dcum.T
    xdb = xd.astype(BF16)
    ri = lax.broadcasted_iota(jnp.int32, (L, L), 0)
    ci = lax.broadcasted_iota(jnp.int32, (L, L), 1)
    keep = (ri >= ci) if fwd else (ri <= ci)
    left = lax.broadcasted_iota(jnp.int32, (L, LANE), 1) < SSM_HEAD_DIM
    parts = []
    for jp in range(HEADS_PER_GROUP // 2):
        xpair = xdb[:, jp * LANE:(jp + 1) * LANE]
        prod = []
        for j in (2 * jp, 2 * jp + 1):
            seg = jnp.where(keep, dcum[:, j:j + 1] - dct[j:j + 1, :], -jnp.inf)
            gm = (cb * jnp.exp(seg)).astype(BF16)
            prod.append(jnp.dot(gm, xpair, preferred_element_type=F32))
        parts.append(jnp.where(left, prod[0], prod[1]))
    return y + jnp.concatenate(parts, axis=1), xs


def _ssd_kernel(xf_ref, bf_ref, cf_ref, dtf_ref, xb_ref, bb_ref, cb_ref, dtb_ref, sel_ref, tri_ref,
                aneg_ref, exp_ref, dsk_ref, yf_ref, yb_ref, stf_scr, stb_scr, *, ctx_chunks):
    s = pl.program_id(2)

    @pl.when(s == 0)
    def _():
        stf_scr[...] = jnp.zeros_like(stf_scr)
        stb_scr[...] = jnp.zeros_like(stb_scr)

    yf, xs = _ssd_direction(xf_ref, bf_ref, cf_ref, dtf_ref, sel_ref.at[0], tri_ref.at[0], aneg_ref.at[0],
                            exp_ref, stf_scr, True)
    yb, _ = _ssd_direction(xb_ref, bb_ref, cb_ref, dtb_ref, sel_ref.at[1], tri_ref.at[1], aneg_ref.at[1],
                           exp_ref, stb_scr, False)

    @pl.when(s >= ctx_chunks)
    def _():
        yf_ref[...] = yf + dsk_ref[...] * xs
        yb_ref[...] = yb


def _ssd(u, dt, a_log, d_skip, n_lat):
    bsz, n_total, _ = u.shape
    L = SSM_CHUNK
    nct, ncl = n_total // L, n_lat // L
    ncc = nct - ncl
    hpg = HEADS_PER_GROUP
    sel = np.zeros((2, SSM_GROUPS, LANE, LANE), np.float32)
    for d in range(2):
        for g in range(SSM_GROUPS):
            for j in range(hpg):
                sel[d, g, DT_LANE + d * SSM_HEADS + g * hpg + j, j] = 1.0
    tri = np.stack([np.tril(np.ones((L, L), np.float32)), np.triu(np.ones((L, L), np.float32))])
    expand = np.zeros((LANE, GROUP_W), np.float32)
    for j in range(hpg):
        expand[j, j * SSM_HEAD_DIM:(j + 1) * SSM_HEAD_DIM] = 1.0
    aneg = -jnp.exp(a_log.astype(F32)).reshape(2, SSM_GROUPS, 1, hpg)
    aneg = jnp.pad(aneg, ((0, 0), (0, 0), (0, 0), (0, LANE - hpg)))
    dsk = jnp.repeat(d_skip.astype(F32), SSM_HEAD_DIM).reshape(1, SSM_INNER)

    def cf(s):
        return (s + ncl) % nct

    def cbk(s):
        return nct - 1 - s

    def chunk_specs(cmap):
        return [pl.BlockSpec((None, L, GROUP_W), lambda b, g, s: (b, cmap(s), g)),
                pl.BlockSpec((None, L, LANE), lambda b, g, s: (b, cmap(s), SSM_INNER // LANE + g)),
                pl.BlockSpec((None, L, LANE), lambda b, g, s: (b, cmap(s), (SSM_INNER + SSM_GN) // LANE + g)),
                pl.BlockSpec((None, L, LANE), lambda b, g, s: (b, cmap(s), 0))]

    in_specs = chunk_specs(cf) + chunk_specs(cbk) + [
        pl.BlockSpec((2, None, LANE, LANE), lambda b, g, s: (0, g, 0, 0)),
        pl.BlockSpec((2, L, L), lambda b, g, s: (0, 0, 0)),
        pl.BlockSpec((2, None, 1, LANE), lambda b, g, s: (0, g, 0, 0)),
        pl.BlockSpec((LANE, GROUP_W), lambda b, g, s: (0, 0)),
        pl.BlockSpec((1, GROUP_W), lambda b, g, s: (0, g))]
    out_specs = [pl.BlockSpec((None, L, GROUP_W), lambda b, g, s: (b, jnp.maximum(s - ncc, 0), g)),
                 pl.BlockSpec((None, L, GROUP_W), lambda b, g, s: (b, jnp.minimum(cbk(s), ncl - 1), g))]
    return pl.pallas_call(
        functools.partial(_ssd_kernel, ctx_chunks=ncc),
        grid=(bsz, SSM_GROUPS, nct),
        in_specs=in_specs,
        out_specs=out_specs,
        out_shape=[jax.ShapeDtypeStruct((bsz, n_lat, SSM_INNER), F32)] * 2,
        scratch_shapes=[pltpu.VMEM((SSM_STATE, GROUP_W), F32), pltpu.VMEM((SSM_STATE, GROUP_W), F32)],
        compiler_params=_cparams(("parallel", "parallel", "arbitrary"), 32),
        name="ssd",
    )(u, u, u, dt, u, u, u, dt, jnp.asarray(sel), jnp.asarray(tri), aneg, jnp.asarray(expand), dsk)


def _rope(blk, cos_ref, s1_ref, s2_ref):
    return (blk * cos_ref[...] + pltpu.roll(blk, LANE - ROPE_HALF, 1) * s1_ref[...]
            + pltpu.roll(blk, ROPE_HALF, 1) * s2_ref[...])


def _qproj_kernel(qc_ref, g_ref, w_ref, cos_ref, s1_ref, s2_ref, q_ref):
    y = (_rms(qc_ref[...]) * g_ref[...]).astype(BF16)
    for h in range(MLA_HEADS):
        qh = jnp.dot(y, w_ref[:, h * HEAD_PAD:(h + 1) * HEAD_PAD], preferred_element_type=F32) * MLA_SCALE
        q_ref[:, h * HEAD_PAD:h * HEAD_PAD + LANE] = qh[:, :LANE].astype(BF16)
        q_ref[:, h * HEAD_PAD + LANE:(h + 1) * HEAD_PAD] = _rope(qh[:, LANE:], cos_ref, s1_ref, s2_ref).astype(BF16)


def _kvproj_kernel(kvc_ref, kr_ref, g_ref, w_ref, cos_ref, s1_ref, s2_ref, k_ref, v_ref):
    y = (_rms(kvc_ref[...]) * g_ref[...]).astype(BF16)
    kr = _rope(kr_ref[...], cos_ref, s1_ref, s2_ref).astype(BF16)
    for h in range(MLA_HEADS):
        kv = jnp.dot(y, w_ref[:, h * HEAD_PAD:(h + 1) * HEAD_PAD], preferred_element_type=F32)
        k_ref[:, h * HEAD_PAD:h * HEAD_PAD + LANE] = kv[:, :QK_NOPE].astype(BF16)
        k_ref[:, h * HEAD_PAD + LANE:(h + 1) * HEAD_PAD] = kr
        v_ref[:, h * V_HEAD:(h + 1) * V_HEAD] = kv[:, QK_NOPE:].astype(BF16)


def _rope_tables(n_lat, n_total):
    rows = n_lat // GRID_W
    row = np.repeat(np.arange(rows, dtype=np.float32), GRID_W)
    col = np.tile(np.arange(GRID_W, dtype=np.float32), rows)
    inv = jnp.asarray(ROPE_THETA, F32) ** (-jnp.arange(ROPE_AXIS_FREQS, dtype=F32) / ROPE_AXIS_FREQS)
    ang = jnp.concatenate([jnp.asarray(row)[:, None] * inv, jnp.asarray(col)[:, None] * inv], axis=-1)
    ang = jnp.pad(ang, ((0, n_total - n_lat), (0, 0)))
    cos, sin = jnp.cos(ang), jnp.sin(ang)
    z32 = jnp.zeros_like(sin)
    z64 = jnp.zeros((n_total, LANE - QK_ROPE), F32)
    return (jnp.concatenate([cos, cos, z64], axis=1),
            jnp.concatenate([-sin, z32, z64], axis=1),
            jnp.concatenate([z32, sin, z64], axis=1))


def _qproj(p, gain, w, tables, n_lat):
    bsz = p.shape[0]
    tm = _tile(n_lat, 512)
    tab = pl.BlockSpec((tm, LANE), lambda b, i: (i, 0))
    return pl.pallas_call(
        _qproj_kernel,
        grid=(bsz, n_lat // tm),
        in_specs=[pl.BlockSpec((None, tm, Q_LORA), lambda b, i: (b, i, P_QC // Q_LORA)),
                  pl.BlockSpec((1, Q_LORA), lambda b, i: (0, 0)),
                  pl.BlockSpec((Q_LORA, MLA_HEADS * HEAD_PAD), lambda b, i: (0, 0)),
                  tab, tab, tab],
        out_specs=pl.BlockSpec((None, tm, MLA_HEADS * HEAD_PAD), lambda b, i: (b, i, 0)),
        out_shape=jax.ShapeDtypeStruct((bsz, n_lat, MLA_HEADS * HEAD_PAD), BF16),
        compiler_params=_cparams(("parallel", "parallel"), 40),
        name="qproj",
    )(p, gain, w, *tables)


def _kvproj(p, gain, w, tables):
    bsz, n_total, _ = p.shape
    tm = _tile(n_total, 256)
    tab = pl.BlockSpec((tm, LANE), lambda b, i: (i, 0))
    return pl.pallas_call(
        _kvproj_kernel,
        grid=(bsz, n_total // tm),
        in_specs=[pl.BlockSpec((None, tm, KV_LORA), lambda b, i: (b, i, P_KVC // KV_LORA)),
                  pl.BlockSpec((None, tm, LANE), lambda b, i: (b, i, P_KR // LANE)),
                  pl.BlockSpec((1, KV_LORA), lambda b, i: (0, 0)),
                  pl.BlockSpec((KV_LORA, MLA_HEADS * HEAD_PAD), lambda b, i: (0, 0)),
                  tab, tab, tab],
        out_specs=[pl.BlockSpec((None, tm, MLA_HEADS * HEAD_PAD), lambda b, i: (b, i, 0)),
                   pl.BlockSpec((None, tm, MLA_OUT), lambda b, i: (b, i, 0))],
        out_shape=[jax.ShapeDtypeStruct((bsz, n_total, MLA_HEADS * HEAD_PAD), BF16),
                   jax.ShapeDtypeStruct((bsz, n_total, MLA_OUT), BF16)],
        compiler_params=_cparams(("parallel", "parallel"), 40),
        name="kvproj",
    )(p, p, gain, w, *tables)


def _attn_kernel(q_ref, k_ref, v_ref, o_ref):
    s = lax.dot_general(q_ref[...], k_ref[...], (((1,), (1,)), ((), ())), preferred_element_type=F32)
    m = jnp.max(s, axis=-1, keepdims=True)
    p = jnp.exp(s - m)
    l = jnp.sum(p, axis=-1, keepdims=True)
    o = jnp.dot(p.astype(BF16), v_ref[...], preferred_element_type=F32)
    o_ref[...] = (o / l).astype(BF16)


def _attention(q, k, v):
    bsz, n_lat, _ = q.shape
    n_total = k.shape[1]
    tq = _tile(n_lat, 256)
    return pl.pallas_call(
        _attn_kernel,
        grid=(bsz, MLA_HEADS, n_lat // tq),
        in_specs=[pl.BlockSpec((None, tq, HEAD_PAD), lambda b, h, i: (b, i, h)),
                  pl.BlockSpec((None, n_total, HEAD_PAD), lambda b, h, i: (b, 0, h)),
                  pl.BlockSpec((None, n_total, V_HEAD), lambda b, h, i: (b, 0, h))],
        out_specs=pl.BlockSpec((None, tq, V_HEAD), lambda b, h, i: (b, i, h)),
        out_shape=jax.ShapeDtypeStruct((bsz, n_lat, MLA_OUT), BF16),
        compiler_params=_cparams(("parallel", "parallel", "parallel"), 48),
        name="attn",
    )(q, k, v)


def _outproj_kernel(h_ref, yf_ref, yb_ref, z_ref, o_ref, gain_ref, gate_ref, w_ref, out_ref, lhs_scr):
    @pl.when(pl.program_id(2) == 0)
    def _():
        g = (yf_ref[...] + yb_ref[...]) * jax.nn.silu(z_ref[...])
        gw = SSM_INNER // SSM_GROUPS
        for gi in range(SSM_GROUPS):
            sl = slice(gi * gw, (gi + 1) * gw)
            lhs_scr[:, sl] = (_rms(g[:, sl]) * gain_ref[:, sl]).astype(BF16)
        lhs_scr[:, SSM_INNER:] = o_ref[...]

    mix = jnp.dot(lhs_scr[...], w_ref[...], preferred_element_type=F32)
    out_ref[...] = h_ref[...] + gate_ref[...] * mix


def _outproj(h, yf, yb, p, o, gain, mods, layer, w):
    bsz, n, d = h.shape
    kdim = w.shape[0]
    tm, tn = _tile(n, 512), _tile(d, 512)
    return pl.pallas_call(
        _outproj_kernel,
        grid=(bsz, n // tm, d // tn),
        in_specs=[pl.BlockSpec((None, tm, tn), lambda b, i, j: (b, i, j)),
                  pl.BlockSpec((None, tm, SSM_INNER), lambda b, i, j: (b, i, 0)),
                  pl.BlockSpec((None, tm, SSM_INNER), lambda b, i, j: (b, i, 0)),
                  pl.BlockSpec((None, tm, SSM_INNER), lambda b, i, j: (b, i, P_Z // SSM_INNER)),
                  pl.BlockSpec((None, tm, MLA_OUT), lambda b, i, j: (b, i, 0)),
                  pl.BlockSpec((1, SSM_INNER), lambda b, i, j: (0, 0)),
                  pl.BlockSpec((None, 1, tn), lambda b, i, j: ((layer * MOD_ROWS + b) * N_MOD + 5, 0, j)),
                  pl.BlockSpec((kdim, tn), lambda b, i, j: (0, j))],
        out_specs=pl.BlockSpec((None, tm, tn), lambda b, i, j: (b, i, j)),
        out_shape=jax.ShapeDtypeStruct((bsz, n, d), F32),
        scratch_shapes=[pltpu.VMEM((tm, kdim), BF16)],
        compiler_params=_cparams(("parallel", "parallel", "arbitrary"), 56),
        name="outproj",
    )(h, yf, yb, p, o, gain, mods, w)


def _pool_kernel(cur_ref, prev_ref, next_ref, sh_ref, sc_ref, gate_ref, w_ref, ps_ref, out_ref, *, n):
    i = pl.program_id(1)
    tm, d = cur_ref.shape
    pg = d // len(POOL_WINDOWS)
    cur = cur_ref[...]
    xe = jnp.concatenate([prev_ref[...], cur, next_ref[...]], axis=0)
    xe = _rms(xe) * (1.0 + sc_ref[...]) + sh_ref[...]
    pos = i * tm - HALO + lax.broadcasted_iota(jnp.int32, (tm + 2 * HALO, 1), 0)
    xe = jnp.where(jnp.logical_and(pos >= 0, pos < n), xe, 0.0)
    t = i * tm + lax.broadcasted_iota(jnp.int32, (tm, 1), 0)
    for gi, win in enumerate(POOL_WINDOWS):
        x = xe[:, gi * pg:(gi + 1) * pg]
        acc = x[0:tm + 2 * HALO - 1] + x[1:tm + 2 * HALO]
        span, off = 2, 1
        while span < win:
            rows = acc.shape[0] - span
            acc = acc[0:rows] + acc[span:span + rows]
            off += span // 2
            span *= 2
        wsum = acc[HALO - off:HALO - off + tm]
        cnt = (jnp.minimum(t + win // 2, n) - jnp.maximum(t - win // 2, 0)).astype(F32)
        pooled = (wsum / cnt - xe[HALO:HALO + tm, gi * pg:(gi + 1) * pg]).astype(BF16)
        mix = jnp.dot(pooled, w_ref[gi], preferred_element_type=F32) * ps_ref[:, gi * pg:(gi + 1) * pg]
        out_ref[:, gi * pg:(gi + 1) * pg] = cur[:, gi * pg:(gi + 1) * pg] + gate_ref[:, gi * pg:(gi + 1) * pg] * mix


def _pool(h, mods, layer, w, scale):
    bsz, n, d = h.shape
    tm = _tile(n, 256)
    pg = d // len(POOL_WINDOWS)
    hb = tm // HALO
    last_hb = n // HALO - 1
    return pl.pallas_call(
        functools.partial(_pool_kernel, n=n),
        grid=(bsz, n // tm),
        in_specs=[pl.BlockSpec((None, tm, d), lambda b, i: (b, i, 0)),
                  pl.BlockSpec((None, HALO, d), lambda b, i: (b, jnp.maximum(i * hb - 1, 0), 0)),
                  pl.BlockSpec((None, HALO, d), lambda b, i: (b, jnp.minimum((i + 1) * hb, last_hb), 0)),
                  _mod_spec(d, layer, 3, None), _mod_spec(d, layer, 4, None), _mod_spec(d, layer, 5, None),
                  pl.BlockSpec((len(POOL_WINDOWS), pg, pg), lambda b, i: (0, 0, 0)),
                  pl.BlockSpec((1, d), lambda b, i: (0, 0))],
        out_specs=pl.BlockSpec((None, tm, d), lambda b, i: (b, i, 0)),
        out_shape=jax.ShapeDtypeStruct((bsz, n, d), F32),
        compiler_params=_cparams(("parallel", "parallel"), 40),
        name="pool",
    )(h, h, h, mods, mods, mods, w, scale.reshape(1, d))


def _prep_w_in(w_in):
    q_c, kv_c, k_r, z, xbc, dtr = jnp.split(
        w_in, np.cumsum((Q_LORA, KV_LORA, QK_ROPE, SSM_INNER, SSM_CONV_CH))[:].tolist(), axis=-1)
    w = jnp.concatenate([z, xbc, q_c, kv_c, k_r, dtr], axis=-1)
    return jnp.pad(w, ((0, 0), (0, P_COLS - w.shape[1]))).astype(BF16)


def _prep_w_uq(w_uq):
    w = w_uq.reshape(Q_LORA, MLA_HEADS, QK_NOPE + QK_ROPE)
    w = jnp.pad(w, ((0, 0), (0, 0), (0, HEAD_PAD - QK_NOPE - QK_ROPE)))
    return w.reshape(Q_LORA, MLA_HEADS * HEAD_PAD).astype(BF16)


def kernel(x, c, ctx, c_ctx, mod_w, mod_b, ffn_w_gate, ffn_w_up, ffn_w_down, w_in, conv_w, conv_b, dt_bias, a_log,
           d_skip, ssm_norm, q_norm, kv_norm, w_uq, w_ukv, w_out, pool_w, pool_scale, final_norm):
    bsz, n, d = x.shape
    n_ctx = ctx.shape[1]
    n_total = n + n_ctx
    depth = mod_w.shape[0]
    assert bsz + 1 <= MOD_ROWS

    cond = jnp.concatenate([c, c_ctx[None, :], jnp.zeros((MOD_ROWS - bsz - 1, d), F32)], axis=0)
    mods = _adaln(cond, mod_w, mod_b).reshape(depth * MOD_ROWS * N_MOD, 1, d)
    wg, wu, wd = ffn_w_gate.astype(BF16), ffn_w_up.astype(BF16), ffn_w_down.astype(BF16)

    h, hc = x, ctx
    for l in range(depth):
        j = l // 2
        last = l == depth - 1
        if l % 2 == 0:
            h = _ffn(h, mods, l, 0, None, wg, wu, wd)
            hc = _ffn(hc, mods, l, 0, bsz, wg, wu, wd)
            w_in_p = _prep_w_in(w_in[j])
            p = _inproj(h, mods, l, None, w_in_p, n_total, 0)
            p = _inproj(hc, mods, l, bsz, w_in_p, n_total, n, prev=p)
            conv_w8 = jnp.pad(conv_w[j], ((0, HALO - SSM_CONV), (0, 0)))
            dtb = jnp.pad(dt_bias[j].reshape(1, 2 * SSM_HEADS).astype(F32), ((0, 0), (DT_LANE, 0)))
            u, dt = _conv(p, conv_w8, conv_b[j].reshape(1, SSM_CONV_CH), dtb, n)
            yf, yb = _ssd(u, dt, a_log[j], d_skip[j], n)
            tables = _rope_tables(n, n_total)
            q = _qproj(p, q_norm[j].reshape(1, Q_LORA), _prep_w_uq(w_uq[j]), tables, n)
            k, v = _kvproj(p, kv_norm[j].reshape(1, KV_LORA), w_ukv[j].astype(BF16), tables)
            o = _attention(q, k, v)
            h = _outproj(h, yf, yb, p, o, ssm_norm[j].reshape(1, SSM_INNER), mods, l, w_out[j].astype(BF16))
        else:
            h = _ffn(h, mods, l, 0, None, wg, wu, wd)
            h = _pool(h, mods, l, pool_w[j].astype(BF16), pool_scale[j])
        h = _ffn(h, mods, l, 1, None, wg, wu, wd, final_gain=final_norm if last else None)
    return h
```

```python
import functools

import jax
import jax.numpy as jnp
import numpy as np
from jax import lax
from jax.experimental import pallas as pl
from jax.experimental.pallas import tpu as pltpu

F32 = jnp.float32
BF16 = jnp.bfloat16

EPS = 1e-6
N_MOD = 9
MOD_ROWS = 8
GRID_W = 64

SSM_HEADS = 32
SSM_HEAD_DIM = 64
SSM_INNER = SSM_HEADS * SSM_HEAD_DIM
SSM_GROUPS = 4
SSM_STATE = 128
SSM_CONV = 5
SSM_CHUNK = 128
SSM_GN = SSM_GROUPS * SSM_STATE
SSM_CONV_CH = SSM_INNER + 2 * SSM_GN
HEADS_PER_GROUP = SSM_HEADS // SSM_GROUPS
GROUP_W = HEADS_PER_GROUP * SSM_HEAD_DIM
XBC_GROUP_W = GROUP_W + 2 * SSM_STATE

MLA_HEADS = 16
Q_LORA = 512
KV_LORA = 512
QK_NOPE = 128
QK_ROPE = 64
V_HEAD = 128
MLA_OUT = MLA_HEADS * V_HEAD
MLA_SCALE = (QK_NOPE + QK_ROPE) ** -0.5
Q_SCALE = MLA_SCALE * 1.4426950408889634
ROPE_THETA = 10000.0
ROPE_HALF = QK_ROPE // 2
ROPE_AXIS_FREQS = QK_ROPE // 4
HEAD_PAD = 256

LANE = 128
HALO = 8
HALO16 = 16

P_Z = 0
P_XBC = P_Z + SSM_INNER
P_QC = P_XBC + SSM_CONV_CH
P_KVC = P_QC + Q_LORA
DT_LANE = QK_ROPE

POOL_WINDOWS = (2, 4, 8, 16)


def _tile(dim, pref):
    t = min(dim, pref)
    while dim % t:
        t //= 2
    return t


def _cparams(sem, vmem_mb):
    return pltpu.CompilerParams(dimension_semantics=sem, vmem_limit_bytes=vmem_mb << 20)


def _rms(x):
    return x * lax.rsqrt(jnp.mean(x * x, axis=-1, keepdims=True) + EPS)


def _mod_spec(d, layer, k, row):
    if row is None:
        return pl.BlockSpec((None, 1, d), lambda b, *_: ((layer * MOD_ROWS + b) * N_MOD + k, 0, 0))
    return pl.BlockSpec((None, 1, d), lambda b, *_: ((layer * MOD_ROWS + row) * N_MOD + k, 0, 0))


def _adaln_kernel(c_ref, w_ref, b_ref, o_ref):
    a = jax.nn.silu(c_ref[...]).astype(BF16)
    o_ref[...] = jnp.dot(a, w_ref[...].astype(BF16), preferred_element_type=F32) + b_ref[...]


def _adaln(cond, mod_w, mod_b):
    depth, d, nd = mod_w.shape
    tn = _tile(nd, 1024)
    return pl.pallas_call(
        _adaln_kernel,
        grid=(depth, nd // tn),
        in_specs=[pl.BlockSpec((MOD_ROWS, d), lambda l, j: (0, 0)),
                  pl.BlockSpec((None, d, tn), lambda l, j: (l, 0, j)),
                  pl.BlockSpec((None, 1, tn), lambda l, j: (l, 0, j))],
        out_specs=pl.BlockSpec((None, MOD_ROWS, tn), lambda l, j: (l, 0, j)),
        out_shape=jax.ShapeDtypeStruct((depth, MOD_ROWS, nd), F32),
        compiler_params=_cparams(("parallel", "parallel"), 40),
        name="adaln",
    )(cond, mod_w, mod_b.reshape(depth, 1, nd))


def _ffn_kernel(*refs, final_norm):
    if final_norm:
        h_ref, sh_ref, sc_ref, gt_ref, wg_ref, wu_ref, wd_ref, fn_ref, o_ref, u_scr, acc_scr = refs
    else:
        h_ref, sh_ref, sc_ref, gt_ref, wg_ref, wu_ref, wd_ref, o_ref, u_scr, acc_scr = refs
    j = pl.program_id(2)

    @pl.when(j == 0)
    def _():
        u = _rms(h_ref[...]) * (1.0 + sc_ref[...]) + sh_ref[...]
        u_scr[...] = u.astype(BF16)
        acc_scr[...] = jnp.zeros_like(acc_scr)

    u = u_scr[...]
    g = jnp.dot(u, wg_ref[...], preferred_element_type=F32)
    up = jnp.dot(u, wu_ref[...], preferred_element_type=F32)
    a = (jax.nn.silu(g) * up).astype(BF16)
    acc_scr[...] += jnp.dot(a, wd_ref[...], preferred_element_type=F32)

    @pl.when(j == pl.num_programs(2) - 1)
    def _():
        out = h_ref[...] + (0.5 * gt_ref[...]) * acc_scr[...]
        if final_norm:
            out = _rms(out) * fn_ref[...]
        o_ref[...] = out


def _ffn(h, mods, layer, half, row, wg, wu, wd, final_gain=None):
    bsz, n, d = h.shape
    f = wg.shape[-1]
    tm, tf = _tile(n, 512), _tile(f, 512)
    k0 = 6 * half
    in_specs = [pl.BlockSpec((None, tm, d), lambda b, i, j: (b, i, 0)),
                _mod_spec(d, layer, k0, row), _mod_spec(d, layer, k0 + 1, row), _mod_spec(d, layer, k0 + 2, row),
                pl.BlockSpec((None, None, d, tf), lambda b, i, j: (layer, half, 0, j)),
                pl.BlockSpec((None, None, d, tf), lambda b, i, j: (layer, half, 0, j)),
                pl.BlockSpec((None, None, tf, d), lambda b, i, j: (layer, half, j, 0))]
    args = [h, mods, mods, mods, wg, wu, wd]
    if final_gain is not None:
        in_specs.append(pl.BlockSpec((1, d), lambda b, i, j: (0, 0)))
        args.append(final_gain.reshape(1, d))
    return pl.pallas_call(
        functools.partial(_ffn_kernel, final_norm=final_gain is not None),
        grid=(bsz, n // tm, f // tf),
        in_specs=in_specs,
        out_specs=pl.BlockSpec((None, tm, d), lambda b, i, j: (b, i, 0)),
        out_shape=jax.ShapeDtypeStruct((bsz, n, d), F32),
        scratch_shapes=[pltpu.VMEM((tm, d), BF16), pltpu.VMEM((tm, d), F32)],
        compiler_params=_cparams(("parallel", "parallel", "arbitrary"), 48),
        name="ffn",
    )(*args)


def _inproj_kernel(h_ref, sh_ref, sc_ref, w_ref, wkd_ref, *rest):
    o_ref, kd_ref, u_scr = rest[-3:]

    @pl.when(pl.program_id(2) == 0)
    def _():
        u = (_rms(h_ref[...]) * (1.0 + sc_ref[...]) + sh_ref[...]).astype(BF16)
        u_scr[...] = u
        kd_ref[...] = jnp.dot(u, wkd_ref[...], preferred_element_type=F32)

    o_ref[...] = jnp.dot(u_scr[...], w_ref[...], preferred_element_type=F32).astype(BF16)


def _inproj(h, mods, layer, row, w, wkd, n_total, row_off, prev=None):
    bsz, n, d = h.shape
    cols = w.shape[1]
    tm, tn = _tile(n, 1024), _tile(cols, 1536)
    off = row_off // tm
    in_specs = [pl.BlockSpec((None, tm, d), lambda b, i, j: (b, i, 0)),
                _mod_spec(d, layer, 3, row), _mod_spec(d, layer, 4, row),
                pl.BlockSpec((d, tn), lambda b, i, j: (0, j)),
                pl.BlockSpec((d, LANE), lambda b, i, j: (0, 0))]
    args = [h, mods, mods, w, wkd]
    aliases = {}
    if prev is not None:
        in_specs += [pl.BlockSpec(memory_space=pl.ANY)] * 2
        args += list(prev)
        aliases = {5: 0, 6: 1}
    return pl.pallas_call(
        _inproj_kernel,
        grid=(bsz, n // tm, cols // tn),
        in_specs=in_specs,
        out_specs=[pl.BlockSpec((None, tm, tn), lambda b, i, j: (b, off + i, j)),
                   pl.BlockSpec((None, tm, LANE), lambda b, i, j: (b, off + i, 0))],
        out_shape=[jax.ShapeDtypeStruct((bsz, n_total, cols), BF16),
                   jax.ShapeDtypeStruct((bsz, n_total, LANE), F32)],
        scratch_shapes=[pltpu.VMEM((tm, d), BF16)],
        input_output_aliases=aliases,
        compiler_params=_cparams(("parallel", "parallel", "arbitrary"), 48),
        name="inproj",
    )(*args)


def _exact_dot(lhs, rhs, lhs_split):
    val = lhs if lhs_split else rhs
    acc = None
    for i in range(3):
        piece = val.astype(BF16)
        if lhs_split:
            d = jnp.dot(piece, rhs, preferred_element_type=F32)
        else:
            d = jnp.dot(lhs, piece, preferred_element_type=F32)
        acc = d if acc is None else acc + d
        if i < 2:
            val = val - piece.astype(F32)
    return acc


def _conv_kernel(cur_ref, prev_ref, next_ref, dtr_ref, w_ref, b_ref, dtb_ref, aneg_ref, tri_ref,
                 u_ref, dtc_ref, *, lat_tiles):
    t = pl.program_id(1)
    tc = cur_ref.shape[0]
    first = jnp.logical_or(t == 0, t == lat_tiles)
    last = jnp.logical_or(t == lat_tiles - 1, t == pl.num_programs(1) - 1)
    prev = jnp.where(first, 0.0, prev_ref[...].astype(F32))
    nxt = jnp.where(last, 0.0, next_ref[...].astype(F32))
    xe = jnp.concatenate([prev, cur_ref[...].astype(F32), nxt], axis=0)
    w = w_ref[...]
    acc = b_ref[...] + w[0:1, :] * xe[HALO16 - 2:HALO16 - 2 + tc, :]
    for k in range(1, SSM_CONV):
        acc = acc + w[k:k + 1, :] * xe[HALO16 - 2 + k:HALO16 - 2 + k + tc, :]
    u_ref[...] = jax.nn.silu(acc).astype(BF16)

    @pl.when(pl.program_id(2) == 0)
    def _():
        dt = jax.nn.softplus(dtr_ref[...] + dtb_ref[...])
        dtc_ref[:, :LANE] = dt
        a = dt * aneg_ref[...]
        fwd_lane = lax.broadcasted_iota(jnp.int32, (SSM_CHUNK, LANE), 1) < DT_LANE + SSM_HEADS
        for r in range(tc // SSM_CHUNK):
            rows = slice(r * SSM_CHUNK, (r + 1) * SSM_CHUNK)
            pre = _exact_dot(tri_ref[0], a[rows], lhs_split=False)
            suf = _exact_dot(tri_ref[1], a[rows], lhs_split=False)
            dtc_ref[rows, LANE:] = jnp.where(fwd_lane, pre, suf)


def _conv(p, kd, conv_w8, conv_b, dt_bias_row, a_log, n_lat):
    bsz, n_total, _ = p.shape
    tc = 256
    cw = 1024
    L = SSM_CHUNK
    aneg = jnp.pad(-jnp.exp(a_log.astype(F32)).reshape(1, 2 * SSM_HEADS), ((0, 0), (DT_LANE, 0)))
    tri = jnp.asarray(np.stack([np.tril(np.ones((L, L), np.float32)), np.triu(np.ones((L, L), np.float32))]), BF16)
    assert n_lat % tc == 0 and n_total % tc == 0 and SSM_CONV_CH % cw == 0 and P_XBC % cw == 0
    c0 = P_XBC // cw
    hb = tc // HALO16
    last_hb = n_total // HALO16 - 1
    return pl.pallas_call(
        functools.partial(_conv_kernel, lat_tiles=n_lat // tc),
        grid=(bsz, n_total // tc, SSM_CONV_CH // cw),
        in_specs=[pl.BlockSpec((None, tc, cw), lambda b, t, c: (b, t, c0 + c)),
                  pl.BlockSpec((None, HALO16, cw), lambda b, t, c: (b, jnp.maximum(t * hb - 1, 0), c0 + c)),
                  pl.BlockSpec((None, HALO16, cw), lambda b, t, c: (b, jnp.minimum((t + 1) * hb, last_hb), c0 + c)),
                  pl.BlockSpec((None, tc, LANE), lambda b, t, c: (b, t, 0)),
                  pl.BlockSpec((HALO, cw), lambda b, t, c: (0, c)),
                  pl.BlockSpec((1, cw), lambda b, t, c: (0, c)),
                  pl.BlockSpec((1, LANE), lambda b, t, c: (0, 0)),
                  pl.BlockSpec((1, LANE), lambda b, t, c: (0, 0)),
                  pl.BlockSpec((2, L, L), lambda b, t, c: (0, 0, 0))],
        out_specs=[pl.BlockSpec((None, tc, cw), lambda b, t, c: (b, t, c)),
                   pl.BlockSpec((None, tc, 2 * LANE), lambda b, t, c: (b, t, 0))],
        out_shape=[jax.ShapeDtypeStruct((bsz, n_total, SSM_CONV_CH), BF16),
                   jax.ShapeDtypeStruct((bsz, n_total, 2 * LANE), F32)],
        compiler_params=_cparams(("parallel", "parallel", "arbitrary"), 32),
        name="conv",
    )(p, p, p, kd, conv_w8, conv_b, dt_bias_row, aneg, tri)


def _ssd_scalars(dtc_ref, exp_ref, shift, fwd):
    L = SSM_CHUNK
    dtg = pltpu.roll(dtc_ref[:, :LANE], shift, 1)
    dcum = pltpu.roll(dtc_ref[:, LANE:], shift, 1)
    edc = jnp.exp(dcum)
    dct = dcum.T
    dtt = dtg.T
    tot = L - 1 if fwd else 0
    wt = dtt * jnp.exp(dct[:, tot:tot + 1] - dct)
    etot = jnp.broadcast_to(edc[tot:tot + 1, :], (HALO, LANE))
    etot = _exact_dot(etot, exp_ref[...], lhs_split=True)[0:1, :]
    return dcum, edc, dct, dtt, wt, etot


def _ssd_direction(xbc_ref, scalars, st_ref, fwd):
    L = SSM_CHUNK
    dcum, edc, dct, dtt, wt, etot = scalars
    xs = xbc_ref[:, :GROUP_W]
    bm = xbc_ref[:, GROUP_W:GROUP_W + SSM_STATE]
    cm = xbc_ref[:, GROUP_W + SSM_STATE:]
    cb = lax.dot_general(cm, bm, (((1,), (1,)), ((), ())), preferred_element_type=F32)
    cm32 = cm.astype(F32)
    bt32 = bm.astype(F32).T
    st = st_ref[...]
    st16 = st.astype(BF16)
    ri = lax.broadcasted_iota(jnp.int32, (L, L), 0)
    ci = lax.broadcasted_iota(jnp.int32, (L, L), 1)
    keep = (ri >= ci) if fwd else (ri <= ci)
    left = lax.broadcasted_iota(jnp.int32, (L, LANE), 1) < SSM_HEAD_DIM
    ys, snew = [], []
    for jp in range(HEADS_PER_GROUP // 2):
        xpair = xs[:, jp * LANE:(jp + 1) * LANE]
        rhs = jnp.concatenate([xpair, st16[:, jp * LANE:(jp + 1) * LANE]], axis=0)
        yp, sp = [], []
        for j in (2 * jp, 2 * jp + 1):
            seg = jnp.where(keep, dcum[:, j:j + 1] - dct[j:j + 1, :], -jnp.inf)
            gm = cb * jnp.exp(seg) * dtt[j:j + 1, :]
            ce = cm32 * edc[:, j:j + 1]
            lhs = jnp.concatenate([gm.astype(BF16), ce.astype(BF16)], axis=1)
            yp.append(jnp.dot(lhs, rhs, preferred_element_type=F32))
            btj = (bt32 * wt[j:j + 1, :]).astype(BF16)
            sp.append(jnp.dot(btj, xpair, preferred_element_type=F32))
        ys.append(jnp.where(left, yp[0], yp[1]))
        snew.append(jnp.where(left, sp[0], sp[1]))
    st_ref[...] = etot * st + jnp.concatenate(snew, axis=1)
    return jnp.concatenate(ys, axis=1), xs


def _ssd_kernel(xf_ref, dtf_ref, xb_ref, dtb_ref, exp_ref, dsk_ref, yf_ref, yb_ref, stf_scr, stb_scr, *,
                ctx_chunks):
    s = pl.program_id(2)

    @pl.when(s == 0)
    def _():
        stf_scr[...] = jnp.zeros_like(stf_scr)
        stb_scr[...] = jnp.zeros_like(stb_scr)

    shift_f = LANE - DT_LANE - pl.program_id(1) * HEADS_PER_GROUP
    sc_f = _ssd_scalars(dtf_ref, exp_ref, shift_f, True)
    sc_b = _ssd_scalars(dtb_ref, exp_ref, shift_f - SSM_HEADS, False)
    yf, xs = _ssd_direction(xf_ref, sc_f, stf_scr, True)
    yb, _ = _ssd_direction(xb_ref, sc_b, stb_scr, False)

    @pl.when(s >= ctx_chunks)
    def _():
        yf_ref[...] = yf + dsk_ref[...] * xs.astype(F32)
        yb_ref[...] = yb


def _ssd(u, dtc, d_skip, n_lat):
    bsz, n_total, _ = u.shape
    L = SSM_CHUNK
    nct, ncl = n_total // L, n_lat // L
    ncc = nct - ncl
    expand = np.zeros((LANE, GROUP_W), np.float32)
    for j in range(HEADS_PER_GROUP):
        expand[j, j * SSM_HEAD_DIM:(j + 1) * SSM_HEAD_DIM] = 1.0
    dsk = jnp.repeat(d_skip.astype(F32), SSM_HEAD_DIM).reshape(1, SSM_INNER)

    def cf(s):
        return (s + ncl) % nct

    def cbk(s):
        return nct - 1 - s

    def chunk_specs(cmap):
        return [pl.BlockSpec((None, L, XBC_GROUP_W), lambda b, g, s: (b, cmap(s), g)),
                pl.BlockSpec((None, L, 2 * LANE), lambda b, g, s: (b, cmap(s), 0))]

    in_specs = chunk_specs(cf) + chunk_specs(cbk) + [
        pl.BlockSpec((LANE, GROUP_W), lambda b, g, s: (0, 0)),
        pl.BlockSpec((1, GROUP_W), lambda b, g, s: (0, g))]
    out_specs = [pl.BlockSpec((None, L, GROUP_W), lambda b, g, s: (b, jnp.maximum(s - ncc, 0), g)),
                 pl.BlockSpec((None, L, GROUP_W), lambda b, g, s: (b, jnp.minimum(cbk(s), ncl - 1), g))]
    return pl.pallas_call(
        functools.partial(_ssd_kernel, ctx_chunks=ncc),
        grid=(bsz, SSM_GROUPS, nct),
        in_specs=in_specs,
        out_specs=out_specs,
        out_shape=[jax.ShapeDtypeStruct((bsz, n_lat, SSM_INNER), F32)] * 2,
        scratch_shapes=[pltpu.VMEM((SSM_STATE, GROUP_W), F32), pltpu.VMEM((SSM_STATE, GROUP_W), F32)],
        compiler_params=_cparams(("parallel", "parallel", "arbitrary"), 32),
        name="ssd",
    )(u, dtc, u, dtc, jnp.asarray(expand, BF16), dsk)


def _rope(blk, cos_ref, s1_ref, s2_ref):
    return (blk * cos_ref[...] + pltpu.roll(blk, LANE - ROPE_HALF, 1) * s1_ref[...]
            + pltpu.roll(blk, ROPE_HALF, 1) * s2_ref[...])


def _qproj_kernel(qc_ref, g_ref, w_ref, cos_ref, s1_ref, s2_ref, q_ref):
    y = (_rms(qc_ref[...].astype(F32)) * g_ref[...]).astype(BF16)
    for h in range(MLA_HEADS):
        qh = jnp.dot(y, w_ref[:, h * HEAD_PAD:(h + 1) * HEAD_PAD], preferred_element_type=F32) * Q_SCALE
        q_ref[:, h * HEAD_PAD:h * HEAD_PAD + LANE] = qh[:, :LANE].astype(BF16)
        q_ref[:, h * HEAD_PAD + LANE:(h + 1) * HEAD_PAD] = _rope(qh[:, LANE:], cos_ref, s1_ref, s2_ref).astype(BF16)


def _kvproj_kernel(kvc_ref, kr_ref, g_ref, w_ref, cos_ref, s1_ref, s2_ref, k_ref, v_ref):
    y = (_rms(kvc_ref[...].astype(F32)) * g_ref[...]).astype(BF16)
    kr = _rope(kr_ref[...], cos_ref, s1_ref, s2_ref).astype(BF16)
    for h in range(MLA_HEADS):
        kv = jnp.dot(y, w_ref[:, h * HEAD_PAD:(h + 1) * HEAD_PAD], preferred_element_type=F32)
        k_ref[:, h * HEAD_PAD:h * HEAD_PAD + LANE] = kv[:, :QK_NOPE].astype(BF16)
        k_ref[:, h * HEAD_PAD + LANE:(h + 1) * HEAD_PAD] = kr
        v_ref[:, h * V_HEAD:(h + 1) * V_HEAD] = kv[:, QK_NOPE:].astype(BF16)


def _rope_tables(n_lat, n_total):
    rows = n_lat // GRID_W
    row = np.repeat(np.arange(rows, dtype=np.float32), GRID_W)
    col = np.tile(np.arange(GRID_W, dtype=np.float32), rows)
    inv = jnp.asarray(ROPE_THETA, F32) ** (-jnp.arange(ROPE_AXIS_FREQS, dtype=F32) / ROPE_AXIS_FREQS)
    ang = jnp.concatenate([jnp.asarray(row)[:, None] * inv, jnp.asarray(col)[:, None] * inv], axis=-1)
    ang = jnp.pad(ang, ((0, n_total - n_lat), (0, 0)))
    cos, sin = jnp.cos(ang), jnp.sin(ang)
    z32 = jnp.zeros_like(sin)
    z64 = jnp.zeros((n_total, LANE - QK_ROPE), F32)
    return (jnp.concatenate([cos, cos, z64], axis=1),
            jnp.concatenate([-sin, z32, z64], axis=1),
            jnp.concatenate([z32, sin, z64], axis=1))


def _qproj(p, gain, w, tables, n_lat):
    bsz = p.shape[0]
    tm = _tile(n_lat, 512)
    tab = pl.BlockSpec((tm, LANE), lambda b, i: (i, 0))
    return pl.pallas_call(
        _qproj_kernel,
        grid=(bsz, n_lat // tm),
        in_specs=[pl.BlockSpec((None, tm, Q_LORA), lambda b, i: (b, i, P_QC // Q_LORA)),
                  pl.BlockSpec((1, Q_LORA), lambda b, i: (0, 0)),
                  pl.BlockSpec((Q_LORA, MLA_HEADS * HEAD_PAD), lambda b, i: (0, 0)),
                  tab, tab, tab],
        out_specs=pl.BlockSpec((None, tm, MLA_HEADS * HEAD_PAD), lambda b, i: (b, i, 0)),
        out_shape=jax.ShapeDtypeStruct((bsz, n_lat, MLA_HEADS * HEAD_PAD), BF16),
        compiler_params=_cparams(("parallel", "parallel"), 40),
        name="qproj",
    )(p, gain, w, *tables)


def _kvproj(p, kd, gain, w, tables):
    bsz, n_total, _ = p.shape
    tm = _tile(n_total, 256)
    tab = pl.BlockSpec((tm, LANE), lambda b, i: (i, 0))
    return pl.pallas_call(
        _kvproj_kernel,
        grid=(bsz, n_total // tm),
        in_specs=[pl.BlockSpec((None, tm, KV_LORA), lambda b, i: (b, i, P_KVC // KV_LORA)),
                  pl.BlockSpec((None, tm, LANE), lambda b, i: (b, i, 0)),
                  pl.BlockSpec((1, KV_LORA), lambda b, i: (0, 0)),
                  pl.BlockSpec((KV_LORA, MLA_HEADS * HEAD_PAD), lambda b, i: (0, 0)),
                  tab, tab, tab],
        out_specs=[pl.BlockSpec((None, tm, MLA_HEADS * HEAD_PAD), lambda b, i: (b, i, 0)),
                   pl.BlockSpec((None, tm, MLA_OUT), lambda b, i: (b, i, 0))],
        out_shape=[jax.ShapeDtypeStruct((bsz, n_total, MLA_HEADS * HEAD_PAD), BF16),
                   jax.ShapeDtypeStruct((bsz, n_total, MLA_OUT), BF16)],
        compiler_params=_cparams(("parallel", "parallel"), 40),
        name="kvproj",
    )(p, kd, gain, w, *tables)


def _attn_kernel(q_ref, k_ref, v_ref, o_ref, *, sub):
    q = q_ref[...]
    nk = k_ref.shape[0]
    m = l = acc = None
    bounds = []
    for k0 in range(0, nk, sub):
        k1 = nk if nk - k0 < 2 * sub else k0 + sub
        bounds.append((k0, k1))
        if k1 == nk:
            break

    def scores(b):
        return lax.dot_general(q, k_ref[b[0]:b[1], :], (((1,), (1,)), ((), ())), preferred_element_type=F32)

    s_next = scores(bounds[0])
    for bi, (k0, k1) in enumerate(bounds):
        s = s_next
        if bi + 1 < len(bounds):
            s_next = scores(bounds[bi + 1])
        ms = jnp.max(s, axis=-1, keepdims=True)
        if m is None:
            m = ms
            p = jnp.exp2(s - m)
            l = jnp.sum(p, axis=-1, keepdims=True)
            acc = jnp.dot(p.astype(BF16), v_ref[k0:k1, :], preferred_element_type=F32)
        else:
            mn = jnp.maximum(m, ms)
            alpha = jnp.exp2(m - mn)
            p = jnp.exp2(s - mn)
            l = alpha * l + jnp.sum(p, axis=-1, keepdims=True)
            acc = alpha * acc + jnp.dot(p.astype(BF16), v_ref[k0:k1, :], preferred_element_type=F32)
            m = mn
    o_ref[...] = (acc / l).astype(BF16)


def _attention(q, k, v):
    bsz, n_lat, _ = q.shape
    n_total = k.shape[1]
    tq = _tile(n_lat, 512)
    return pl.pallas_call(
        functools.partial(_attn_kernel, sub=1024),
        grid=(bsz, MLA_HEADS, n_lat // tq),
        in_specs=[pl.BlockSpec((None, tq, HEAD_PAD), lambda b, h, i: (b, i, h)),
                  pl.BlockSpec((None, n_total, HEAD_PAD), lambda b, h, i: (b, 0, h)),
                  pl.BlockSpec((None, n_total, V_HEAD), lambda b, h, i: (b, 0, h))],
        out_specs=pl.BlockSpec((None, tq, V_HEAD), lambda b, h, i: (b, i, h)),
        out_shape=jax.ShapeDtypeStruct((bsz, n_lat, MLA_OUT), BF16),
        compiler_params=_cparams(("parallel", "parallel", "parallel"), 48),
        name="attn",
    )(q, k, v)


def _outproj_kernel(h_ref, yf_ref, yb_ref, z_ref, o_ref, gain_ref, gate_ref, w_ref, out_ref):
    gw = SSM_INNER // SSM_GROUPS
    mix = jnp.dot(o_ref[...], w_ref[SSM_INNER:, :], preferred_element_type=F32)
    for gi in range(SSM_GROUPS):
        sl = slice(gi * gw, (gi + 1) * gw)
        g = (yf_ref[:, sl] + yb_ref[:, sl]) * jax.nn.silu(z_ref[:, sl].astype(F32))
        lhs = (_rms(g) * gain_ref[:, sl]).astype(BF16)
        mix = mix + jnp.dot(lhs, w_ref[sl, :], preferred_element_type=F32)
    out_ref[...] = h_ref[...] + gate_ref[...] * mix


def _outproj(h, yf, yb, p, o, gain, mods, layer, w):
    bsz, n, d = h.shape
    kdim = w.shape[0]
    tm = _tile(n, 256)
    return pl.pallas_call(
        _outproj_kernel,
        grid=(bsz, n // tm),
        in_specs=[pl.BlockSpec((None, tm, d), lambda b, i: (b, i, 0)),
                  pl.BlockSpec((None, tm, SSM_INNER), lambda b, i: (b, i, 0)),
                  pl.BlockSpec((None, tm, SSM_INNER), lambda b, i: (b, i, 0)),
                  pl.BlockSpec((None, tm, SSM_INNER), lambda b, i: (b, i, P_Z // SSM_INNER)),
                  pl.BlockSpec((None, tm, MLA_OUT), lambda b, i: (b, i, 0)),
                  pl.BlockSpec((1, SSM_INNER), lambda b, i: (0, 0)),
                  _mod_spec(d, layer, 5, None),
                  pl.BlockSpec((kdim, d), lambda b, i: (0, 0), pipeline_mode=pl.Buffered(1))],
        out_specs=pl.BlockSpec((None, tm, d), lambda b, i: (b, i, 0)),
        out_shape=jax.ShapeDtypeStruct((bsz, n, d), F32),
        compiler_params=_cparams(("parallel", "parallel"), 56),
        name="outproj",
    )(h, yf, yb, p, o, gain, mods, w)


def _pool_kernel(cur_ref, prev_ref, next_ref, sh_ref, sc_ref, gate_ref, w_ref, ps_ref, out_ref, *, n):
    i = pl.program_id(1)
    tm, d = cur_ref.shape
    pg = d // len(POOL_WINDOWS)
    cur = cur_ref[...]
    xe = jnp.concatenate([prev_ref[...], cur, next_ref[...]], axis=0)
    xe = _rms(xe) * (1.0 + sc_ref[...]) + sh_ref[...]
    pos = i * tm - HALO + lax.broadcasted_iota(jnp.int32, (tm + 2 * HALO, 1), 0)
    xe = jnp.where(jnp.logical_and(pos >= 0, pos < n), xe, 0.0)
    t = i * tm + lax.broadcasted_iota(jnp.int32, (tm, 1), 0)
    for gi, win in enumerate(POOL_WINDOWS):
        x = xe[:, gi * pg:(gi + 1) * pg]
        acc = x[0:tm + 2 * HALO - 1] + x[1:tm + 2 * HALO]
        span, off = 2, 1
        while span < win:
            rows = acc.shape[0] - span
            acc = acc[0:rows] + acc[span:span + rows]
            off += span // 2
            span *= 2
        wsum = acc[HALO - off:HALO - off + tm]
        cnt = (jnp.minimum(t + win // 2, n) - jnp.maximum(t - win // 2, 0)).astype(F32)
        pooled = (wsum / cnt - xe[HALO:HALO + tm, gi * pg:(gi + 1) * pg]).astype(BF16)
        mix = jnp.dot(pooled, w_ref[gi], preferred_element_type=F32) * ps_ref[:, gi * pg:(gi + 1) * pg]
        out_ref[:, gi * pg:(gi + 1) * pg] = cur[:, gi * pg:(gi + 1) * pg] + gate_ref[:, gi * pg:(gi + 1) * pg] * mix


def _pool(h, mods, layer, w, scale):
    bsz, n, d = h.shape
    tm = _tile(n, 256)
    pg = d // len(POOL_WINDOWS)
    hb = tm // HALO
    last_hb = n // HALO - 1
    return pl.pallas_call(
        functools.partial(_pool_kernel, n=n),
        grid=(bsz, n // tm),
        in_specs=[pl.BlockSpec((None, tm, d), lambda b, i: (b, i, 0)),
                  pl.BlockSpec((None, HALO, d), lambda b, i: (b, jnp.maximum(i * hb - 1, 0), 0)),
                  pl.BlockSpec((None, HALO, d), lambda b, i: (b, jnp.minimum((i + 1) * hb, last_hb), 0)),
                  _mod_spec(d, layer, 3, None), _mod_spec(d, layer, 4, None), _mod_spec(d, layer, 5, None),
                  pl.BlockSpec((len(POOL_WINDOWS), pg, pg), lambda b, i: (0, 0, 0)),
                  pl.BlockSpec((1, d), lambda b, i: (0, 0))],
        out_specs=pl.BlockSpec((None, tm, d), lambda b, i: (b, i, 0)),
        out_shape=jax.ShapeDtypeStruct((bsz, n, d), F32),
        compiler_params=_cparams(("parallel", "parallel"), 40),
        name="pool",
    )(h, h, h, mods, mods, mods, w, scale.reshape(1, d))


def _regroup_xbc(a):
    parts = []
    for g in range(SSM_GROUPS):
        parts += [a[..., g * GROUP_W:(g + 1) * GROUP_W],
                  a[..., SSM_INNER + g * SSM_STATE:SSM_INNER + (g + 1) * SSM_STATE],
                  a[..., SSM_INNER + SSM_GN + g * SSM_STATE:SSM_INNER + SSM_GN + (g + 1) * SSM_STATE]]
    return jnp.concatenate(parts, axis=-1)


def _prep_w_in(w_in):
    q_c, kv_c, k_r, z, xbc, dtr = jnp.split(
        w_in, np.cumsum((Q_LORA, KV_LORA, QK_ROPE, SSM_INNER, SSM_CONV_CH))[:].tolist(), axis=-1)
    w = jnp.concatenate([z, _regroup_xbc(xbc), q_c, kv_c], axis=-1)
    return w.astype(BF16), jnp.concatenate([k_r, dtr], axis=-1).astype(BF16)


def _prep_w_uq(w_uq):
    w = w_uq.reshape(Q_LORA, MLA_HEADS, QK_NOPE + QK_ROPE)
    w = jnp.pad(w, ((0, 0), (0, 0), (0, HEAD_PAD - QK_NOPE - QK_ROPE)))
    return w.reshape(Q_LORA, MLA_HEADS * HEAD_PAD).astype(BF16)


def kernel(x, c, ctx, c_ctx, mod_w, mod_b, ffn_w_gate, ffn_w_up, ffn_w_down, w_in, conv_w, conv_b, dt_bias, a_log,
           d_skip, ssm_norm, q_norm, kv_norm, w_uq, w_ukv, w_out, pool_w, pool_scale, final_norm):
    bsz, n, d = x.shape
    n_ctx = ctx.shape[1]
    n_total = n + n_ctx
    depth = mod_w.shape[0]
    assert bsz + 1 <= MOD_ROWS

    cond = jnp.concatenate([c, c_ctx[None, :], jnp.zeros((MOD_ROWS - bsz - 1, d), F32)], axis=0)
    mods = _adaln(cond, mod_w, mod_b).reshape(depth * MOD_ROWS * N_MOD, 1, d)
    wg, wu, wd = ffn_w_gate.astype(BF16), ffn_w_up.astype(BF16), ffn_w_down.astype(BF16)

    h, hc = x, ctx
    for l in range(depth):
        j = l // 2
        last = l == depth - 1
        if l % 2 == 0:
            h = _ffn(h, mods, l, 0, None, wg, wu, wd)
            hc = _ffn(hc, mods, l, 0, bsz, wg, wu, wd)
            w_in_p, w_kd = _prep_w_in(w_in[j])
            pk = _inproj(h, mods, l, None, w_in_p, w_kd, n_total, 0)
            p, kd = _inproj(hc, mods, l, bsz, w_in_p, w_kd, n_total, n, prev=pk)
            conv_w8 = jnp.pad(_regroup_xbc(conv_w[j]), ((0, HALO - SSM_CONV), (0, 0)))
            dtb = jnp.pad(dt_bias[j].reshape(1, 2 * SSM_HEADS).astype(F32), ((0, 0), (DT_LANE, 0)))
            u, dtc = _conv(p, kd, conv_w8, _regroup_xbc(conv_b[j]).reshape(1, SSM_CONV_CH), dtb, a_log[j], n)
            yf, yb = _ssd(u, dtc, d_skip[j], n)
            tables = _rope_tables(n, n_total)
            q = _qproj(p, q_norm[j].reshape(1, Q_LORA), _prep_w_uq(w_uq[j]), tables, n)
            k, v = _kvproj(p, kd, kv_norm[j].reshape(1, KV_LORA), w_ukv[j].astype(BF16), tables)
            o = _attention(q, k, v)
            h = _outproj(h, yf, yb, p, o, ssm_norm[j].reshape(1, SSM_INNER), mods, l, w_out[j].astype(BF16))
        else:
            h = _ffn(h, mods, l, 0, None, wg, wu, wd)
            h = _pool(h, mods, l, pool_w[j].astype(BF16), pool_scale[j])
        h = _ffn(h, mods, l, 1, None, wg, wu, wd, final_gain=final_norm if last else None)
    return h
```

```python
import functools

import jax
import jax.numpy as jnp
import numpy as np
from jax import lax
from jax.experimental import pallas as pl
from jax.experimental.pallas import tpu as pltpu

F32 = jnp.float32
BF16 = jnp.bfloat16

EPS = 1e-6
N_MOD = 9
MOD_ROWS = 8
GRID_W = 64

SSM_HEADS = 32
SSM_HEAD_DIM = 64
SSM_INNER = SSM_HEADS * SSM_HEAD_DIM
SSM_GROUPS = 4
SSM_STATE = 128
SSM_CONV = 5
SSM_CHUNK = 128
SSM_GN = SSM_GROUPS * SSM_STATE
SSM_CONV_CH = SSM_INNER + 2 * SSM_GN
HEADS_PER_GROUP = SSM_HEADS // SSM_GROUPS
GROUP_W = HEADS_PER_GROUP * SSM_HEAD_DIM
XBC_GROUP_W = GROUP_W + 2 * SSM_STATE

MLA_HEADS = 16
Q_LORA = 512
KV_LORA = 512
QK_NOPE = 128
QK_ROPE = 64
V_HEAD = 128
MLA_OUT = MLA_HEADS * V_HEAD
MLA_SCALE = (QK_NOPE + QK_ROPE) ** -0.5
Q_SCALE = MLA_SCALE * 1.4426950408889634
ROPE_THETA = 10000.0
ROPE_HALF = QK_ROPE // 2
ROPE_AXIS_FREQS = QK_ROPE // 4
HEAD_PAD = 256

LANE = 128
HALO = 8
HALO16 = 16

P_Z = 0
P_XBC = P_Z + SSM_INNER
P_QC = P_XBC + SSM_CONV_CH
P_KVC = P_QC + Q_LORA
DT_LANE = QK_ROPE

POOL_WINDOWS = (2, 4, 8, 16)


def _tile(dim, pref):
    t = min(dim, pref)
    while dim % t:
        t //= 2
    return t


def _cparams(sem, vmem_mb):
    return pltpu.CompilerParams(dimension_semantics=sem, vmem_limit_bytes=vmem_mb << 20)


def _rms(x):
    return x * lax.rsqrt(jnp.mean(x * x, axis=-1, keepdims=True) + EPS)


def _mod_spec(d, layer, k, row):
    if row is None:
        return pl.BlockSpec((None, 1, d), lambda b, *_: ((layer * MOD_ROWS + b) * N_MOD + k, 0, 0))
    return pl.BlockSpec((None, 1, d), lambda b, *_: ((layer * MOD_ROWS + row) * N_MOD + k, 0, 0))


def _adaln_kernel(c_ref, w_ref, b_ref, o_ref):
    a = jax.nn.silu(c_ref[...]).astype(BF16)
    o_ref[...] = jnp.dot(a, w_ref[...].astype(BF16), preferred_element_type=F32) + b_ref[...]


def _adaln(cond, mod_w, mod_b):
    depth, d, nd = mod_w.shape
    tn = _tile(nd, 1024)
    return pl.pallas_call(
        _adaln_kernel,
        grid=(depth, nd // tn),
        in_specs=[pl.BlockSpec((MOD_ROWS, d), lambda l, j: (0, 0)),
                  pl.BlockSpec((None, d, tn), lambda l, j: (l, 0, j)),
                  pl.BlockSpec((None, 1, tn), lambda l, j: (l, 0, j))],
        out_specs=pl.BlockSpec((None, MOD_ROWS, tn), lambda l, j: (l, 0, j)),
        out_shape=jax.ShapeDtypeStruct((depth, MOD_ROWS, nd), F32),
        compiler_params=_cparams(("parallel", "parallel"), 40),
        name="adaln",
    )(cond, mod_w, mod_b.reshape(depth, 1, nd))


def _ffn_kernel(*refs, final_norm):
    if final_norm:
        h_ref, sh_ref, sc_ref, gt_ref, wg_ref, wu_ref, wd_ref, fn_ref, o_ref, u_scr = refs
    else:
        h_ref, sh_ref, sc_ref, gt_ref, wg_ref, wu_ref, wd_ref, o_ref, u_scr = refs
    j = pl.program_id(2)

    @pl.when(j == 0)
    def _():
        u = _rms(h_ref[...]) * (1.0 + sc_ref[...]) + sh_ref[...]
        u_scr[...] = u.astype(BF16)
        o_ref[...] = jnp.zeros_like(o_ref)

    u = u_scr[...]
    g = jnp.dot(u, wg_ref[...], preferred_element_type=F32)
    up = jnp.dot(u, wu_ref[...], preferred_element_type=F32)
    a = (jax.nn.silu(g) * up).astype(BF16)
    o_ref[...] += jnp.dot(a, wd_ref[...], preferred_element_type=F32)

    @pl.when(j == pl.num_programs(2) - 1)
    def _():
        out = h_ref[...] + (0.5 * gt_ref[...]) * o_ref[...]
        if final_norm:
            out = _rms(out) * fn_ref[...]
        o_ref[...] = out


def _ffn(h, mods, layer, half, row, wg, wu, wd, final_gain=None):
    bsz, n, d = h.shape
    f = wg.shape[-1]
    tm, tf = _tile(n, 512), _tile(f, 512)
    k0 = 6 * half
    in_specs = [pl.BlockSpec((None, tm, d), lambda b, i, j: (b, i, 0)),
                _mod_spec(d, layer, k0, row), _mod_spec(d, layer, k0 + 1, row), _mod_spec(d, layer, k0 + 2, row),
                pl.BlockSpec((None, None, d, tf), lambda b, i, j: (layer, half, 0, j)),
                pl.BlockSpec((None, None, d, tf), lambda b, i, j: (layer, half, 0, j)),
                pl.BlockSpec((None, None, tf, d), lambda b, i, j: (layer, half, j, 0))]
    args = [h, mods, mods, mods, wg, wu, wd]
    if final_gain is not None:
        in_specs.append(pl.BlockSpec((1, d), lambda b, i, j: (0, 0)))
        args.append(final_gain.reshape(1, d))
    return pl.pallas_call(
        functools.partial(_ffn_kernel, final_norm=final_gain is not None),
        grid=(bsz, n // tm, f // tf),
        in_specs=in_specs,
        out_specs=pl.BlockSpec((None, tm, d), lambda b, i, j: (b, i, 0)),
        out_shape=jax.ShapeDtypeStruct((bsz, n, d), F32),
        scratch_shapes=[pltpu.VMEM((tm, d), BF16)],
        compiler_params=_cparams(("parallel", "parallel", "arbitrary"), 48),
        name="ffn",
    )(*args)


def _inproj_kernel(h_ref, sh_ref, sc_ref, w_ref, wkd_ref, *rest):
    o_ref, kd_ref, u_scr = rest[-3:]

    @pl.when(pl.program_id(2) == 0)
    def _():
        u = (_rms(h_ref[...]) * (1.0 + sc_ref[...]) + sh_ref[...]).astype(BF16)
        u_scr[...] = u
        kd_ref[...] = jnp.dot(u, wkd_ref[...], preferred_element_type=F32)

    o_ref[...] = jnp.dot(u_scr[...], w_ref[...], preferred_element_type=F32).astype(BF16)


def _inproj(h, mods, layer, row, w, wkd, n_total, row_off, prev=None):
    bsz, n, d = h.shape
    cols = w.shape[1]
    tm, tn = _tile(n, 1024), _tile(cols, 1536)
    off = row_off // tm
    in_specs = [pl.BlockSpec((None, tm, d), lambda b, i, j: (b, i, 0)),
                _mod_spec(d, layer, 3, row), _mod_spec(d, layer, 4, row),
                pl.BlockSpec((d, tn), lambda b, i, j: (0, j)),
                pl.BlockSpec((d, LANE), lambda b, i, j: (0, 0))]
    args = [h, mods, mods, w, wkd]
    aliases = {}
    if prev is not None:
        in_specs += [pl.BlockSpec(memory_space=pl.ANY)] * 2
        args += list(prev)
        aliases = {5: 0, 6: 1}
    return pl.pallas_call(
        _inproj_kernel,
        grid=(bsz, n // tm, cols // tn),
        in_specs=in_specs,
        out_specs=[pl.BlockSpec((None, tm, tn), lambda b, i, j: (b, off + i, j)),
                   pl.BlockSpec((None, tm, LANE), lambda b, i, j: (b, off + i, 0))],
        out_shape=[jax.ShapeDtypeStruct((bsz, n_total, cols), BF16),
                   jax.ShapeDtypeStruct((bsz, n_total, LANE), F32)],
        scratch_shapes=[pltpu.VMEM((tm, d), BF16)],
        input_output_aliases=aliases,
        compiler_params=_cparams(("parallel", "parallel", "arbitrary"), 48),
        name="inproj",
    )(*args)


def _exact_dot(lhs, rhs, lhs_split):
    val = lhs if lhs_split else rhs
    acc = None
    for i in range(3):
        piece = val.astype(BF16)
        if lhs_split:
            d = jnp.dot(piece, rhs, preferred_element_type=F32)
        else:
            d = jnp.dot(lhs, piece, preferred_element_type=F32)
        acc = d if acc is None else acc + d
        if i < 2:
            val = val - piece.astype(F32)
    return acc


def _conv_kernel(cur_ref, prev_ref, next_ref, dtr_ref, w_ref, b_ref, dtb_ref, aneg_ref, tri_ref,
                 u_ref, dtc_ref, *, lat_tiles):
    t = pl.program_id(1)
    tc = cur_ref.shape[0]
    first = jnp.logical_or(t == 0, t == lat_tiles)
    last = jnp.logical_or(t == lat_tiles - 1, t == pl.num_programs(1) - 1)
    prev = jnp.where(first, 0.0, prev_ref[...].astype(F32))
    nxt = jnp.where(last, 0.0, next_ref[...].astype(F32))
    xe = jnp.concatenate([prev, cur_ref[...].astype(F32), nxt], axis=0)
    w = w_ref[...]
    acc = b_ref[...] + w[0:1, :] * xe[HALO16 - 2:HALO16 - 2 + tc, :]
    for k in range(1, SSM_CONV):
        acc = acc + w[k:k + 1, :] * xe[HALO16 - 2 + k:HALO16 - 2 + k + tc, :]
    u_ref[...] = jax.nn.silu(acc).astype(BF16)

    @pl.when(pl.program_id(2) == 0)
    def _():
        dt = jax.nn.softplus(dtr_ref[...] + dtb_ref[...])
        dtc_ref[:, :LANE] = dt
        a = dt * aneg_ref[...]
        fwd_lane = lax.broadcasted_iota(jnp.int32, (SSM_CHUNK, LANE), 1) < DT_LANE + SSM_HEADS
        for r in range(tc // SSM_CHUNK):
            rows = slice(r * SSM_CHUNK, (r + 1) * SSM_CHUNK)
            pre = _exact_dot(tri_ref[0], a[rows], lhs_split=False)
            suf = _exact_dot(tri_ref[1], a[rows], lhs_split=False)
            dtc_ref[rows, LANE:] = jnp.where(fwd_lane, pre, suf)


def _conv(p, kd, conv_w8, conv_b, dt_bias_row, a_log, n_lat):
    bsz, n_total, _ = p.shape
    tc = 256
    cw = 1024
    L = SSM_CHUNK
    aneg = jnp.pad(-jnp.exp(a_log.astype(F32)).reshape(1, 2 * SSM_HEADS), ((0, 0), (DT_LANE, 0)))
    tri = jnp.asarray(np.stack([np.tril(np.ones((L, L), np.float32)), np.triu(np.ones((L, L), np.float32))]), BF16)
    assert n_lat % tc == 0 and n_total % tc == 0 and SSM_CONV_CH % cw == 0 and P_XBC % cw == 0
    c0 = P_XBC // cw
    hb = tc // HALO16
    last_hb = n_total // HALO16 - 1
    return pl.pallas_call(
        functools.partial(_conv_kernel, lat_tiles=n_lat // tc),
        grid=(bsz, n_total // tc, SSM_CONV_CH // cw),
        in_specs=[pl.BlockSpec((None, tc, cw), lambda b, t, c: (b, t, c0 + c)),
                  pl.BlockSpec((None, HALO16, cw), lambda b, t, c: (b, jnp.maximum(t * hb - 1, 0), c0 + c)),
                  pl.BlockSpec((None, HALO16, cw), lambda b, t, c: (b, jnp.minimum((t + 1) * hb, last_hb), c0 + c)),
                  pl.BlockSpec((None, tc, LANE), lambda b, t, c: (b, t, 0)),
                  pl.BlockSpec((HALO, cw), lambda b, t, c: (0, c)),
                  pl.BlockSpec((1, cw), lambda b, t, c: (0, c)),
                  pl.BlockSpec((1, LANE), lambda b, t, c: (0, 0)),
                  pl.BlockSpec((1, LANE), lambda b, t, c: (0, 0)),
                  pl.BlockSpec((2, L, L), lambda b, t, c: (0, 0, 0))],
        out_specs=[pl.BlockSpec((None, tc, cw), lambda b, t, c: (b, t, c)),
                   pl.BlockSpec((None, tc, 2 * LANE), lambda b, t, c: (b, t, 0))],
        out_shape=[jax.ShapeDtypeStruct((bsz, n_total, SSM_CONV_CH), BF16),
                   jax.ShapeDtypeStruct((bsz, n_total, 2 * LANE), F32)],
        compiler_params=_cparams(("parallel", "parallel", "arbitrary"), 32),
        name="conv",
    )(p, p, p, kd, conv_w8, conv_b, dt_bias_row, aneg, tri)


def _ssd_scalars(dtc_ref, exp_ref, fwd):
    L = SSM_CHUNK
    dcum = dtc_ref[:, LANE:]
    edc = jnp.exp(dcum)
    dct = dcum.T
    dtt = dtc_ref[:, :LANE].T
    tot = L - 1 if fwd else 0
    wt = dtt * jnp.exp(dct[:, tot:tot + 1] - dct)
    etot = jnp.broadcast_to(edc[tot:tot + 1, :], (HALO, LANE))
    etot = _exact_dot(etot, exp_ref[...], lhs_split=True)[0:1, :]
    return dcum, edc, dct, dtt, wt, etot


def _ssd_direction(xbc_ref, scalars, st_ref, fwd):
    L = SSM_CHUNK
    dcum, edc, dct, dtt, wt, etot = scalars
    ri = lax.broadcasted_iota(jnp.int32, (L, L), 0)
    ci = lax.broadcasted_iota(jnp.int32, (L, L), 1)
    keep = (ri >= ci) if fwd else (ri <= ci)
    left = lax.broadcasted_iota(jnp.int32, (L, LANE), 1) < SSM_HEAD_DIM
    ys, xss = [], []
    for g in range(SSM_GROUPS):
        c0 = g * XBC_GROUP_W
        xs = xbc_ref[:, c0:c0 + GROUP_W]
        bm = xbc_ref[:, c0 + GROUP_W:c0 + GROUP_W + SSM_STATE]
        cm = xbc_ref[:, c0 + GROUP_W + SSM_STATE:c0 + XBC_GROUP_W]
        cb = lax.dot_general(cm, bm, (((1,), (1,)), ((), ())), preferred_element_type=F32)
        cm32 = cm.astype(F32)
        bt32 = bm.astype(F32).T
        st = st_ref[:, g * GROUP_W:(g + 1) * GROUP_W]
        st16 = st.astype(BF16)
        snew = []
        for jp in range(HEADS_PER_GROUP // 2):
            xpair = xs[:, jp * LANE:(jp + 1) * LANE]
            rhs = jnp.concatenate([xpair, st16[:, jp * LANE:(jp + 1) * LANE]], axis=0)
            yp, sp = [], []
            for j in (2 * jp, 2 * jp + 1):
                hl = DT_LANE + (0 if fwd else SSM_HEADS) + g * HEADS_PER_GROUP + j
                seg = jnp.where(keep, dcum[:, hl:hl + 1] - dct[hl:hl + 1, :], -jnp.inf)
                gm = cb * jnp.exp(seg) * dtt[hl:hl + 1, :]
                ce = cm32 * edc[:, hl:hl + 1]
                lhs = jnp.concatenate([gm.astype(BF16), ce.astype(BF16)], axis=1)
                yp.append(jnp.dot(lhs, rhs, preferred_element_type=F32))
                btj = (bt32 * wt[hl:hl + 1, :]).astype(BF16)
                sp.append(jnp.dot(btj, xpair, preferred_element_type=F32))
            ys.append(jnp.where(left, yp[0], yp[1]))
            snew.append(jnp.where(left, sp[0], sp[1]))
        st_ref[:, g * GROUP_W:(g + 1) * GROUP_W] = (etot[:, g * GROUP_W:(g + 1) * GROUP_W] * st
                                                    + jnp.concatenate(snew, axis=1))
        xss.append(xs)
    return jnp.concatenate(ys, axis=1), jnp.concatenate(xss, axis=1)


def _ssd_kernel(xf_ref, dtf_ref, xb_ref, dtb_ref, exp_ref, dsk_ref, yf_ref, yb_ref, stf_scr, stb_scr, *,
                ctx_chunks):
    s = pl.program_id(1)

    @pl.when(s == 0)
    def _():
        stf_scr[...] = jnp.zeros_like(stf_scr)
        stb_scr[...] = jnp.zeros_like(stb_scr)

    sc_f = _ssd_scalars(dtf_ref, exp_ref.at[0], True)
    sc_b = _ssd_scalars(dtb_ref, exp_ref.at[1], False)
    yf, xs = _ssd_direction(xf_ref, sc_f, stf_scr, True)
    yb, _ = _ssd_direction(xb_ref, sc_b, stb_scr, False)

    @pl.when(s >= ctx_chunks)
    def _():
        yf_ref[...] = yf + dsk_ref[...] * xs.astype(F32)
        yb_ref[...] = yb


def _ssd(u, dtc, d_skip, n_lat):
    bsz, n_total, _ = u.shape
    L = SSM_CHUNK
    nct, ncl = n_total // L, n_lat // L
    ncc = nct - ncl
    expand = np.zeros((2, LANE, SSM_INNER), np.float32)
    for d in range(2):
        for h in range(SSM_HEADS):
            expand[d, DT_LANE + d * SSM_HEADS + h, h * SSM_HEAD_DIM:(h + 1) * SSM_HEAD_DIM] = 1.0
    dsk = jnp.repeat(d_skip.astype(F32), SSM_HEAD_DIM).reshape(1, SSM_INNER)

    def cf(s):
        return (s + ncl) % nct

    def cbk(s):
        return nct - 1 - s

    def chunk_specs(cmap):
        return [pl.BlockSpec((None, L, SSM_CONV_CH), lambda b, s: (b, cmap(s), 0)),
                pl.BlockSpec((None, L, 2 * LANE), lambda b, s: (b, cmap(s), 0))]

    in_specs = chunk_specs(cf) + chunk_specs(cbk) + [
        pl.BlockSpec((2, LANE, SSM_INNER), lambda b, s: (0, 0, 0)),
        pl.BlockSpec((1, SSM_INNER), lambda b, s: (0, 0))]
    out_specs = [pl.BlockSpec((None, L, SSM_INNER), lambda b, s: (b, jnp.maximum(s - ncc, 0), 0)),
                 pl.BlockSpec((None, L, SSM_INNER), lambda b, s: (b, jnp.minimum(cbk(s), ncl - 1), 0))]
    return pl.pallas_call(
        functools.partial(_ssd_kernel, ctx_chunks=ncc),
        grid=(bsz, nct),
        in_specs=in_specs,
        out_specs=out_specs,
        out_shape=[jax.ShapeDtypeStruct((bsz, n_lat, SSM_INNER), F32)] * 2,
        scratch_shapes=[pltpu.VMEM((SSM_STATE, SSM_INNER), F32), pltpu.VMEM((SSM_STATE, SSM_INNER), F32)],
        compiler_params=_cparams(("parallel", "arbitrary"), 32),
        name="ssd",
    )(u, dtc, u, dtc, jnp.asarray(expand, BF16), dsk)


def _rope(blk, cos_ref, s1_ref, s2_ref):
    return (blk * cos_ref[...] + pltpu.roll(blk, LANE - ROPE_HALF, 1) * s1_ref[...]
            + pltpu.roll(blk, ROPE_HALF, 1) * s2_ref[...])


def _qproj_kernel(qc_ref, g_ref, w_ref, cos_ref, s1_ref, s2_ref, q_ref):
    y = (_rms(qc_ref[...].astype(F32)) * g_ref[...]).astype(BF16)
    for h in range(MLA_HEADS):
        qh = jnp.dot(y, w_ref[:, h * HEAD_PAD:(h + 1) * HEAD_PAD], preferred_element_type=F32) * Q_SCALE
        q_ref[:, h * HEAD_PAD:h * HEAD_PAD + LANE] = qh[:, :LANE].astype(BF16)
        q_ref[:, h * HEAD_PAD + LANE:(h + 1) * HEAD_PAD] = _rope(qh[:, LANE:], cos_ref, s1_ref, s2_ref).astype(BF16)


def _kvproj_kernel(kvc_ref, kr_ref, g_ref, w_ref, cos_ref, s1_ref, s2_ref, k_ref, vt_ref):
    y = (_rms(kvc_ref[...].astype(F32)) * g_ref[...]).astype(BF16)
    kr = _rope(kr_ref[...], cos_ref, s1_ref, s2_ref).astype(BF16)
    for h in range(MLA_HEADS):
        kv = jnp.dot(y, w_ref[:, h * HEAD_PAD:(h + 1) * HEAD_PAD], preferred_element_type=F32)
        k_ref[:, h * HEAD_PAD:h * HEAD_PAD + LANE] = kv[:, :QK_NOPE].astype(BF16)
        k_ref[:, h * HEAD_PAD + LANE:(h + 1) * HEAD_PAD] = kr
        vt_ref[h * V_HEAD:(h + 1) * V_HEAD, :] = kv[:, QK_NOPE:].T.astype(BF16)


def _rope_tables(n_lat, n_total):
    rows = n_lat // GRID_W
    row = np.repeat(np.arange(rows, dtype=np.float32), GRID_W)
    col = np.tile(np.arange(GRID_W, dtype=np.float32), rows)
    inv = jnp.asarray(ROPE_THETA, F32) ** (-jnp.arange(ROPE_AXIS_FREQS, dtype=F32) / ROPE_AXIS_FREQS)
    ang = jnp.concatenate([jnp.asarray(row)[:, None] * inv, jnp.asarray(col)[:, None] * inv], axis=-1)
    ang = jnp.pad(ang, ((0, n_total - n_lat), (0, 0)))
    cos, sin = jnp.cos(ang), jnp.sin(ang)
    z32 = jnp.zeros_like(sin)
    z64 = jnp.zeros((n_total, LANE - QK_ROPE), F32)
    return (jnp.concatenate([cos, cos, z64], axis=1),
            jnp.concatenate([-sin, z32, z64], axis=1),
            jnp.concatenate([z32, sin, z64], axis=1))


def _qproj(p, gain, w, tables, n_lat):
    bsz = p.shape[0]
    tm = _tile(n_lat, 512)
    tab = pl.BlockSpec((tm, LANE), lambda b, i: (i, 0))
    return pl.pallas_call(
        _qproj_kernel,
        grid=(bsz, n_lat // tm),
        in_specs=[pl.BlockSpec((None, tm, Q_LORA), lambda b, i: (b, i, P_QC // Q_LORA)),
                  pl.BlockSpec((1, Q_LORA), lambda b, i: (0, 0)),
                  pl.BlockSpec((Q_LORA, MLA_HEADS * HEAD_PAD), lambda b, i: (0, 0)),
                  tab, tab, tab],
        out_specs=pl.BlockSpec((None, tm, MLA_HEADS * HEAD_PAD), lambda b, i: (b, i, 0)),
        out_shape=jax.ShapeDtypeStruct((bsz, n_lat, MLA_HEADS * HEAD_PAD), BF16),
        compiler_params=_cparams(("parallel", "parallel"), 40),
        name="qproj",
    )(p, gain, w, *tables)


def _kvproj(p, kd, gain, w, tables):
    bsz, n_total, _ = p.shape
    tm = _tile(n_total, 256)
    tab = pl.BlockSpec((tm, LANE), lambda b, i: (i, 0))
    return pl.pallas_call(
        _kvproj_kernel,
        grid=(bsz, n_total // tm),
        in_specs=[pl.BlockSpec((None, tm, KV_LORA), lambda b, i: (b, i, P_KVC // KV_LORA)),
                  pl.BlockSpec((None, tm, LANE), lambda b, i: (b, i, 0)),
                  pl.BlockSpec((1, KV_LORA), lambda b, i: (0, 0)),
                  pl.BlockSpec((KV_LORA, MLA_HEADS * HEAD_PAD), lambda b, i: (0, 0)),
                  tab, tab, tab],
        out_specs=[pl.BlockSpec((None, tm, MLA_HEADS * HEAD_PAD), lambda b, i: (b, i, 0)),
                   pl.BlockSpec((None, MLA_OUT, tm), lambda b, i: (b, 0, i))],
        out_shape=[jax.ShapeDtypeStruct((bsz, n_total, MLA_HEADS * HEAD_PAD), BF16),
                   jax.ShapeDtypeStruct((bsz, MLA_OUT, n_total), BF16)],
        compiler_params=_cparams(("parallel", "parallel"), 40),
        name="kvproj",
    )(p, kd, gain, w, *tables)


def _attn_kernel(q_ref, k_ref, vt_ref, o_ref, *, sub):
    q = q_ref[...]
    nk = k_ref.shape[0]
    m = l = acc = None
    bounds = []
    for k0 in range(0, nk, sub):
        k1 = nk if nk - k0 < 2 * sub else k0 + sub
        bounds.append((k0, k1))
        if k1 == nk:
            break

    def scores(b):
        return lax.dot_general(k_ref[b[0]:b[1], :], q, (((1,), (1,)), ((), ())), preferred_element_type=F32)

    s_next = scores(bounds[0])
    for bi, (k0, k1) in enumerate(bounds):
        s = s_next
        if bi + 1 < len(bounds):
            s_next = scores(bounds[bi + 1])
        ms = jnp.max(s, axis=0, keepdims=True)
        if m is None:
            m = ms
            p = jnp.exp2(s - m)
            l = jnp.sum(p, axis=0, keepdims=True)
            acc = jnp.dot(vt_ref[:, k0:k1], p.astype(BF16), preferred_element_type=F32)
        else:
            mn = jnp.maximum(m, ms)
            alpha = jnp.exp2(m - mn)
            p = jnp.exp2(s - mn)
            l = alpha * l + jnp.sum(p, axis=0, keepdims=True)
            acc = alpha * acc + jnp.dot(vt_ref[:, k0:k1], p.astype(BF16), preferred_element_type=F32)
            m = mn
    o_ref[...] = (acc / l).T.astype(BF16)


def _attention(q, k, vt):
    bsz, n_lat, _ = q.shape
    n_total = k.shape[1]
    tq = _tile(n_lat, 1024)
    return pl.pallas_call(
        functools.partial(_attn_kernel, sub=1024),
        grid=(bsz, MLA_HEADS, n_lat // tq),
        in_specs=[pl.BlockSpec((None, tq, HEAD_PAD), lambda b, h, i: (b, i, h)),
                  pl.BlockSpec((None, n_total, HEAD_PAD), lambda b, h, i: (b, 0, h)),
                  pl.BlockSpec((None, V_HEAD, n_total), lambda b, h, i: (b, h, 0))],
        out_specs=pl.BlockSpec((None, tq, V_HEAD), lambda b, h, i: (b, i, h)),
        out_shape=jax.ShapeDtypeStruct((bsz, n_lat, MLA_OUT), BF16),
        compiler_params=_cparams(("parallel", "parallel", "parallel"), 48),
        name="attn",
    )(q, k, vt)


def _outproj_kernel(h_ref, yf_ref, yb_ref, z_ref, o_ref, gain_ref, gate_ref, w_ref, out_ref):
    gw = SSM_INNER // SSM_GROUPS
    mix = jnp.dot(o_ref[...], w_ref[SSM_INNER:, :], preferred_element_type=F32)
    for gi in range(SSM_GROUPS):
        sl = slice(gi * gw, (gi + 1) * gw)
        g = (yf_ref[:, sl] + yb_ref[:, sl]) * jax.nn.silu(z_ref[:, sl].astype(F32))
        lhs = (_rms(g) * gain_ref[:, sl]).astype(BF16)
        mix = mix + jnp.dot(lhs, w_ref[sl, :], preferred_element_type=F32)
    out_ref[...] = h_ref[...] + gate_ref[...] * mix


def _outproj(h, yf, yb, p, o, gain, mods, layer, w):
    bsz, n, d = h.shape
    kdim = w.shape[0]
    tm = _tile(n, 256)
    return pl.pallas_call(
        _outproj_kernel,
        grid=(bsz, n // tm),
        in_specs=[pl.BlockSpec((None, tm, d), lambda b, i: (b, i, 0)),
                  pl.BlockSpec((None, tm, SSM_INNER), lambda b, i: (b, i, 0)),
                  pl.BlockSpec((None, tm, SSM_INNER), lambda b, i: (b, i, 0)),
                  pl.BlockSpec((None, tm, SSM_INNER), lambda b, i: (b, i, P_Z // SSM_INNER)),
                  pl.BlockSpec((None, tm, MLA_OUT), lambda b, i: (b, i, 0)),
                  pl.BlockSpec((1, SSM_INNER), lambda b, i: (0, 0)),
                  _mod_spec(d, layer, 5, None),
                  pl.BlockSpec((kdim, d), lambda b, i: (0, 0), pipeline_mode=pl.Buffered(1))],
        out_specs=pl.BlockSpec((None, tm, d), lambda b, i: (b, i, 0)),
        out_shape=jax.ShapeDtypeStruct((bsz, n, d), F32),
        compiler_params=_cparams(("parallel", "parallel"), 56),
        name="outproj",
    )(h, yf, yb, p, o, gain, mods, w)


def _pool_kernel(cur_ref, prev_ref, next_ref, sh_ref, sc_ref, gate_ref, w_ref, ps_ref, out_ref, *, n):
    i = pl.program_id(1)
    tm, d = cur_ref.shape
    pg = d // len(POOL_WINDOWS)
    cur = cur_ref[...]
    xe = jnp.concatenate([prev_ref[...], cur, next_ref[...]], axis=0)
    xe = _rms(xe) * (1.0 + sc_ref[...]) + sh_ref[...]
    pos = i * tm - HALO + lax.broadcasted_iota(jnp.int32, (tm + 2 * HALO, 1), 0)
    xe = jnp.where(jnp.logical_and(pos >= 0, pos < n), xe, 0.0)
    t = i * tm + lax.broadcasted_iota(jnp.int32, (tm, 1), 0)
    for gi, win in enumerate(POOL_WINDOWS):
        x = xe[:, gi * pg:(gi + 1) * pg]
        acc = x[0:tm + 2 * HALO - 1] + x[1:tm + 2 * HALO]
        span, off = 2, 1
        while span < win:
            rows = acc.shape[0] - span
            acc = acc[0:rows] + acc[span:span + rows]
            off += span // 2
            span *= 2
        wsum = acc[HALO - off:HALO - off + tm]
        cnt = (jnp.minimum(t + win // 2, n) - jnp.maximum(t - win // 2, 0)).astype(F32)
        pooled = (wsum / cnt - xe[HALO:HALO + tm, gi * pg:(gi + 1) * pg]).astype(BF16)
        mix = jnp.dot(pooled, w_ref[gi], preferred_element_type=F32) * ps_ref[:, gi * pg:(gi + 1) * pg]
        out_ref[:, gi * pg:(gi + 1) * pg] = cur[:, gi * pg:(gi + 1) * pg] + gate_ref[:, gi * pg:(gi + 1) * pg] * mix


def _pool(h, mods, layer, w, scale):
    bsz, n, d = h.shape
    tm = _tile(n, 256)
    pg = d // len(POOL_WINDOWS)
    hb = tm // HALO
    last_hb = n // HALO - 1
    return pl.pallas_call(
        functools.partial(_pool_kernel, n=n),
        grid=(bsz, n // tm),
        in_specs=[pl.BlockSpec((None, tm, d), lambda b, i: (b, i, 0)),
                  pl.BlockSpec((None, HALO, d), lambda b, i: (b, jnp.maximum(i * hb - 1, 0), 0)),
                  pl.BlockSpec((None, HALO, d), lambda b, i: (b, jnp.minimum((i + 1) * hb, last_hb), 0)),
                  _mod_spec(d, layer, 3, None), _mod_spec(d, layer, 4, None), _mod_spec(d, layer, 5, None),
                  pl.BlockSpec((len(POOL_WINDOWS), pg, pg), lambda b, i: (0, 0, 0)),
                  pl.BlockSpec((1, d), lambda b, i: (0, 0))],
        out_specs=pl.BlockSpec((None, tm, d), lambda b, i: (b, i, 0)),
        out_shape=jax.ShapeDtypeStruct((bsz, n, d), F32),
        compiler_params=_cparams(("parallel", "parallel"), 40),
        name="pool",
    )(h, h, h, mods, mods, mods, w, scale.reshape(1, d))


def _regroup_xbc(a):
    parts = []
    for g in range(SSM_GROUPS):
        parts += [a[..., g * GROUP_W:(g + 1) * GROUP_W],
                  a[..., SSM_INNER + g * SSM_STATE:SSM_INNER + (g + 1) * SSM_STATE],
                  a[..., SSM_INNER + SSM_GN + g * SSM_STATE:SSM_INNER + SSM_GN + (g + 1) * SSM_STATE]]
    return jnp.concatenate(parts, axis=-1)


def _prep_w_in(w_in):
    q_c, kv_c, k_r, z, xbc, dtr = jnp.split(
        w_in, np.cumsum((Q_LORA, KV_LORA, QK_ROPE, SSM_INNER, SSM_CONV_CH))[:].tolist(), axis=-1)
    w = jnp.concatenate([z, _regroup_xbc(xbc), q_c, kv_c], axis=-1)
    return w.astype(BF16), jnp.concatenate([k_r, dtr], axis=-1).astype(BF16)


def _prep_w_uq(w_uq):
    w = w_uq.reshape(Q_LORA, MLA_HEADS, QK_NOPE + QK_ROPE)
    w = jnp.pad(w, ((0, 0), (0, 0), (0, HEAD_PAD - QK_NOPE - QK_ROPE)))
    return w.reshape(Q_LORA, MLA_HEADS * HEAD_PAD).astype(BF16)


def kernel(x, c, ctx, c_ctx, mod_w, mod_b, ffn_w_gate, ffn_w_up, ffn_w_down, w_in, conv_w, conv_b, dt_bias, a_log,
           d_skip, ssm_norm, q_norm, kv_norm, w_uq, w_ukv, w_out, pool_w, pool_scale, final_norm):
    bsz, n, d = x.shape
    n_ctx = ctx.shape[1]
    n_total = n + n_ctx
    depth = mod_w.shape[0]
    assert bsz + 1 <= MOD_ROWS

    cond = jnp.concatenate([c, c_ctx[None, :], jnp.zeros((MOD_ROWS - bsz - 1, d), F32)], axis=0)
    mods = _adaln(cond, mod_w, mod_b).reshape(depth * MOD_ROWS * N_MOD, 1, d)
    wg, wu, wd = ffn_w_gate.astype(BF16), ffn_w_up.astype(BF16), ffn_w_down.astype(BF16)

    h, hc = x, ctx
    for l in range(depth):
        j = l // 2
        last = l == depth - 1
        if l % 2 == 0:
            h = _ffn(h, mods, l, 0, None, wg, wu, wd)
            hc = _ffn(hc, mods, l, 0, bsz, wg, wu, wd)
            w_in_p, w_kd = _prep_w_in(w_in[j])
            pk = _inproj(h, mods, l, None, w_in_p, w_kd, n_total, 0)
            p, kd = _inproj(hc, mods, l, bsz, w_in_p, w_kd, n_total, n, prev=pk)
            conv_w8 = jnp.pad(_regroup_xbc(conv_w[j]), ((0, HALO - SSM_CONV), (0, 0)))
            dtb = jnp.pad(dt_bias[j].reshape(1, 2 * SSM_HEADS).astype(F32), ((0, 0), (DT_LANE, 0)))
            u, dtc = _conv(p, kd, conv_w8, _regroup_xbc(conv_b[j]).reshape(1, SSM_CONV_CH), dtb, a_log[j], n)
            yf, yb = _ssd(u, dtc, d_skip[j], n)
            tables = _rope_tables(n, n_total)
            q = _qproj(p, q_norm[j].reshape(1, Q_LORA), _prep_w_uq(w_uq[j]), tables, n)
            k, vt = _kvproj(p, kd, kv_norm[j].reshape(1, KV_LORA), w_ukv[j].astype(BF16), tables)
            o = _attention(q, k, vt)
            h = _outproj(h, yf, yb, p, o, ssm_norm[j].reshape(1, SSM_INNER), mods, l, w_out[j].astype(BF16))
        else:
            h = _ffn(h, mods, l, 0, None, wg, wu, wd)
            h = _pool(h, mods, l, pool_w[j].astype(BF16), pool_scale[j])
        h = _ffn(h, mods, l, 1, None, wg, wu, wd, final_gain=final_norm if last else None)
    return h
```

```python
import functools

import jax
import jax.numpy as jnp
import numpy as np
from jax import lax
from jax.experimental import pallas as pl
from jax.experimental.pallas import tpu as pltpu

F32 = jnp.float32
BF16 = jnp.bfloat16

EPS = 1e-6
N_MOD = 9
MOD_ROWS = 8
GRID_W = 64

SSM_HEADS = 32
SSM_HEAD_DIM = 64
SSM_INNER = SSM_HEADS * SSM_HEAD_DIM
SSM_GROUPS = 4
SSM_STATE = 128
SSM_CONV = 5
SSM_CHUNK = 128
SSM_GN = SSM_GROUPS * SSM_STATE
SSM_CONV_CH = SSM_INNER + 2 * SSM_GN
HEADS_PER_GROUP = SSM_HEADS // SSM_GROUPS
GROUP_W = HEADS_PER_GROUP * SSM_HEAD_DIM
XBC_GROUP_W = GROUP_W + 2 * SSM_STATE

MLA_HEADS = 16
Q_LORA = 512
KV_LORA = 512
QK_NOPE = 128
QK_ROPE = 64
V_HEAD = 128
MLA_OUT = MLA_HEADS * V_HEAD
MLA_SCALE = (QK_NOPE + QK_ROPE) ** -0.5
Q_SCALE = MLA_SCALE * 1.4426950408889634
ROPE_THETA = 10000.0
ROPE_HALF = QK_ROPE // 2
ROPE_AXIS_FREQS = QK_ROPE // 4
HEAD_PAD = 256

LANE = 128
HALO = 8
HALO16 = 16

P_Z = 0
P_XBC = P_Z + SSM_INNER
P_QC = P_XBC + SSM_CONV_CH
P_KVC = P_QC + Q_LORA
DT_LANE = QK_ROPE

POOL_WINDOWS = (2, 4, 8, 16)


def _tile(dim, pref):
    t = min(dim, pref)
    while dim % t:
        t //= 2
    return t


def _cparams(sem, vmem_mb):
    return pltpu.CompilerParams(dimension_semantics=sem, vmem_limit_bytes=vmem_mb << 20)


def _rms(x):
    return x * lax.rsqrt(jnp.mean(x * x, axis=-1, keepdims=True) + EPS)


def _mod_spec(d, layer, k, row):
    if row is None:
        return pl.BlockSpec((None, 1, d), lambda b, *_: ((layer * MOD_ROWS + b) * N_MOD + k, 0, 0))
    return pl.BlockSpec((None, 1, d), lambda b, *_: ((layer * MOD_ROWS + row) * N_MOD + k, 0, 0))


def _adaln_kernel(c_ref, w_ref, b_ref, o_ref):
    a = jax.nn.silu(c_ref[...]).astype(BF16)
    o_ref[...] = jnp.dot(a, w_ref[...].astype(BF16), preferred_element_type=F32) + b_ref[...]


def _adaln(cond, mod_w, mod_b):
    depth, d, nd = mod_w.shape
    tn = _tile(nd, 1024)
    return pl.pallas_call(
        _adaln_kernel,
        grid=(depth, nd // tn),
        in_specs=[pl.BlockSpec((MOD_ROWS, d), lambda l, j: (0, 0)),
                  pl.BlockSpec((None, d, tn), lambda l, j: (l, 0, j)),
                  pl.BlockSpec((None, 1, tn), lambda l, j: (l, 0, j))],
        out_specs=pl.BlockSpec((None, MOD_ROWS, tn), lambda l, j: (l, 0, j)),
        out_shape=jax.ShapeDtypeStruct((depth, MOD_ROWS, nd), F32),
        compiler_params=_cparams(("parallel", "parallel"), 40),
        name="adaln",
    )(cond, mod_w, mod_b.reshape(depth, 1, nd))


def _ffn_kernel(*refs, final_norm, n_chunks):
    if final_norm:
        (h_ref, hn_ref, sh_ref, sc_ref, gt_ref, shn_ref, scn_ref, wg_ref, wu_ref, wd_ref, fn_ref,
         o_ref, ua_scr, ub_scr) = refs
    else:
        (h_ref, hn_ref, sh_ref, sc_ref, gt_ref, shn_ref, scn_ref, wg_ref, wu_ref, wd_ref,
         o_ref, ua_scr, ub_scr) = refs
    j = pl.program_id(2)
    t = pl.program_id(0) * pl.num_programs(1) + pl.program_id(1)
    rc = h_ref.shape[0] // n_chunks

    def modnorm(x, sc, sh):
        return (_rms(x) * (1.0 + sc) + sh).astype(BF16)

    @pl.when(jnp.logical_and(t == 0, j == 0))
    def _():
        ua_scr[...] = modnorm(h_ref[...], sc_ref[...], sh_ref[...])

    @pl.when(j == 0)
    def _():
        o_ref[...] = jnp.zeros_like(o_ref)

    def step(u_ref, un_ref):
        r = jnp.minimum(j, n_chunks - 1)
        rows = pl.ds(pl.multiple_of(r * rc, rc), rc)
        un_ref[rows, :] = modnorm(hn_ref[rows, :], scn_ref[...], shn_ref[...])
        u = u_ref[...]
        g = jnp.dot(u, wg_ref[...], preferred_element_type=F32)
        up = jnp.dot(u, wu_ref[...], preferred_element_type=F32)
        a = (jax.nn.silu(g) * up).astype(BF16)
        o_ref[...] += jnp.dot(a, wd_ref[...], preferred_element_type=F32)

    pl.when(t % 2 == 0)(lambda: step(ua_scr, ub_scr))
    pl.when(t % 2 == 1)(lambda: step(ub_scr, ua_scr))

    @pl.when(j == pl.num_programs(2) - 1)
    def _():
        out = h_ref[...] + (0.5 * gt_ref[...]) * o_ref[...]
        if final_norm:
            out = _rms(out) * fn_ref[...]
        o_ref[...] = out


def _ffn(h, mods, layer, half, row, wg, wu, wd, final_gain=None):
    bsz, n, d = h.shape
    f = wg.shape[-1]
    tm, tf = _tile(n, 512), _tile(f, 512)
    nm, nj = n // tm, f // tf
    n_chunks = min(nj, 8)
    assert tm % n_chunks == 0 and (tm // n_chunks) % HALO16 == 0
    k0 = 6 * half

    def nxt(b, i):
        t1 = jnp.minimum(b * nm + i + 1, bsz * nm - 1)
        return t1 // nm, t1 % nm

    def mod_next(k):
        if row is not None:
            return _mod_spec(d, layer, k, row)
        return pl.BlockSpec((None, 1, d), lambda b, i, j: ((layer * MOD_ROWS + nxt(b, i)[0]) * N_MOD + k, 0, 0))

    in_specs = [pl.BlockSpec((None, tm, d), lambda b, i, j: (b, i, 0)),
                pl.BlockSpec((None, tm, d), lambda b, i, j: (*nxt(b, i), 0)),
                _mod_spec(d, layer, k0, row), _mod_spec(d, layer, k0 + 1, row), _mod_spec(d, layer, k0 + 2, row),
                mod_next(k0), mod_next(k0 + 1),
                pl.BlockSpec((None, None, d, tf), lambda b, i, j: (layer, half, 0, j)),
                pl.BlockSpec((None, None, d, tf), lambda b, i, j: (layer, half, 0, j)),
                pl.BlockSpec((None, None, tf, d), lambda b, i, j: (layer, half, j, 0))]
    args = [h, h, mods, mods, mods, mods, mods, wg, wu, wd]
    if final_gain is not None:
        in_specs.append(pl.BlockSpec((1, d), lambda b, i, j: (0, 0)))
        args.append(final_gain.reshape(1, d))
    return pl.pallas_call(
        functools.partial(_ffn_kernel, final_norm=final_gain is not None, n_chunks=n_chunks),
        grid=(bsz, nm, nj),
        in_specs=in_specs,
        out_specs=pl.BlockSpec((None, tm, d), lambda b, i, j: (b, i, 0)),
        out_shape=jax.ShapeDtypeStruct((bsz, n, d), F32),
        scratch_shapes=[pltpu.VMEM((tm, d), BF16), pltpu.VMEM((tm, d), BF16)],
        compiler_params=_cparams(("arbitrary", "arbitrary", "arbitrary"), 56),
        name="ffn",
    )(*args)


def _inproj_kernel(h_ref, sh_ref, sc_ref, w_ref, wkd_ref, *rest):
    o_ref, kd_ref, u_scr = rest[-3:]

    @pl.when(pl.program_id(2) == 0)
    def _():
        u = (_rms(h_ref[...]) * (1.0 + sc_ref[...]) + sh_ref[...]).astype(BF16)
        u_scr[...] = u
        kd_ref[...] = jnp.dot(u, wkd_ref[...], preferred_element_type=F32)

    o_ref[...] = jnp.dot(u_scr[...], w_ref[...], preferred_element_type=F32).astype(BF16)


def _inproj(h, mods, layer, row, w, wkd, n_total, row_off, prev=None):
    bsz, n, d = h.shape
    cols = w.shape[1]
    tm, tn = _tile(n, 1024), _tile(cols, 1536)
    off = row_off // tm
    in_specs = [pl.BlockSpec((None, tm, d), lambda b, i, j: (b, i, 0)),
                _mod_spec(d, layer, 3, row), _mod_spec(d, layer, 4, row),
                pl.BlockSpec((d, tn), lambda b, i, j: (0, j)),
                pl.BlockSpec((d, LANE), lambda b, i, j: (0, 0))]
    args = [h, mods, mods, w, wkd]
    aliases = {}
    if prev is not None:
        in_specs += [pl.BlockSpec(memory_space=pl.ANY)] * 2
        args += list(prev)
        aliases = {5: 0, 6: 1}
    return pl.pallas_call(
        _inproj_kernel,
        grid=(bsz, n // tm, cols // tn),
        in_specs=in_specs,
        out_specs=[pl.BlockSpec((None, tm, tn), lambda b, i, j: (b, off + i, j)),
                   pl.BlockSpec((None, tm, LANE), lambda b, i, j: (b, off + i, 0))],
        out_shape=[jax.ShapeDtypeStruct((bsz, n_total, cols), BF16),
                   jax.ShapeDtypeStruct((bsz, n_total, LANE), F32)],
        scratch_shapes=[pltpu.VMEM((tm, d), BF16)],
        input_output_aliases=aliases,
        compiler_params=_cparams(("parallel", "parallel", "arbitrary"), 48),
        name="inproj",
    )(*args)


def _exact_dot(lhs, rhs, lhs_split):
    val = lhs if lhs_split else rhs
    acc = None
    for i in range(3):
        piece = val.astype(BF16)
        if lhs_split:
            d = jnp.dot(piece, rhs, preferred_element_type=F32)
        else:
            d = jnp.dot(lhs, piece, preferred_element_type=F32)
        acc = d if acc is None else acc + d
        if i < 2:
            val = val - piece.astype(F32)
    return acc


def _conv_kernel(cur_ref, prev_ref, next_ref, dtr_ref, w_ref, b_ref, dtb_ref, aneg_ref, tri_ref,
                 u_ref, dtc_ref, *, lat_tiles):
    t = pl.program_id(1)
    tc = cur_ref.shape[0]
    first = jnp.logical_or(t == 0, t == lat_tiles)
    last = jnp.logical_or(t == lat_tiles - 1, t == pl.num_programs(1) - 1)
    prev = jnp.where(first, 0.0, prev_ref[...].astype(F32))
    nxt = jnp.where(last, 0.0, next_ref[...].astype(F32))
    xe = jnp.concatenate([prev, cur_ref[...].astype(F32), nxt], axis=0)
    w = w_ref[...]
    ext = tc + 2 * HALO16
    acc = b_ref[...] + w[2:3, :] * xe[HALO16:HALO16 + tc, :]
    for k in (0, 1, 3, 4):
        acc = acc + w[k:k + 1, :] * pltpu.roll(xe, (2 - k) % ext, 0)[HALO16:HALO16 + tc, :]
    u_ref[...] = jax.nn.silu(acc).astype(BF16)

    @pl.when(pl.program_id(2) == 0)
    def _():
        dt = jax.nn.softplus(dtr_ref[...] + dtb_ref[...])
        dtc_ref[:, :LANE] = dt
        a = dt * aneg_ref[...]
        fwd_lane = lax.broadcasted_iota(jnp.int32, (SSM_CHUNK, LANE), 1) < DT_LANE + SSM_HEADS
        for r in range(tc // SSM_CHUNK):
            rows = slice(r * SSM_CHUNK, (r + 1) * SSM_CHUNK)
            pre = _exact_dot(tri_ref[0], a[rows], lhs_split=False)
            suf = _exact_dot(tri_ref[1], a[rows], lhs_split=False)
            dtc_ref[rows, LANE:] = jnp.where(fwd_lane, pre, suf)


def _conv(p, kd, conv_w8, conv_b, dt_bias_row, a_log, n_lat):
    bsz, n_total, _ = p.shape
    tc = 256
    cw = 1024
    L = SSM_CHUNK
    aneg = jnp.pad(-jnp.exp(a_log.astype(F32)).reshape(1, 2 * SSM_HEADS), ((0, 0), (DT_LANE, 0)))
    tri = jnp.asarray(np.stack([np.tril(np.ones((L, L), np.float32)), np.triu(np.ones((L, L), np.float32))]), BF16)
    assert n_lat % tc == 0 and n_total % tc == 0 and SSM_CONV_CH % cw == 0 and P_XBC % cw == 0
    c0 = P_XBC // cw
    hb = tc // HALO16
    last_hb = n_total // HALO16 - 1
    return pl.pallas_call(
        functools.partial(_conv_kernel, lat_tiles=n_lat // tc),
        grid=(bsz, n_total // tc, SSM_CONV_CH // cw),
        in_specs=[pl.BlockSpec((None, tc, cw), lambda b, t, c: (b, t, c0 + c)),
                  pl.BlockSpec((None, HALO16, cw), lambda b, t, c: (b, jnp.maximum(t * hb - 1, 0), c0 + c)),
                  pl.BlockSpec((None, HALO16, cw), lambda b, t, c: (b, jnp.minimum((t + 1) * hb, last_hb), c0 + c)),
                  pl.BlockSpec((None, tc, LANE), lambda b, t, c: (b, t, 0)),
                  pl.BlockSpec((HALO, cw), lambda b, t, c: (0, c)),
                  pl.BlockSpec((1, cw), lambda b, t, c: (0, c)),
                  pl.BlockSpec((1, LANE), lambda b, t, c: (0, 0)),
                  pl.BlockSpec((1, LANE), lambda b, t, c: (0, 0)),
                  pl.BlockSpec((2, L, L), lambda b, t, c: (0, 0, 0))],
        out_specs=[pl.BlockSpec((None, tc, cw), lambda b, t, c: (b, t, c)),
                   pl.BlockSpec((None, tc, 2 * LANE), lambda b, t, c: (b, t, 0))],
        out_shape=[jax.ShapeDtypeStruct((bsz, n_total, SSM_CONV_CH), BF16),
                   jax.ShapeDtypeStruct((bsz, n_total, 2 * LANE), F32)],
        compiler_params=_cparams(("parallel", "parallel", "arbitrary"), 32),
        name="conv",
    )(p, p, p, kd, conv_w8, conv_b, dt_bias_row, aneg, tri)


def _ssd_scalars(dtc_ref, exp_ref, fwd):
    L = SSM_CHUNK
    dcum = dtc_ref[:, LANE:]
    edc = jnp.exp(dcum)
    dct = dcum.T
    dtt = dtc_ref[:, :LANE].T
    tot = L - 1 if fwd else 0
    wt = dtt * jnp.exp(dct[:, tot:tot + 1] - dct)
    etot = jnp.broadcast_to(edc[tot:tot + 1, :], (HALO, LANE))
    etot = _exact_dot(etot, exp_ref[...], lhs_split=True)[0:1, :]
    return dcum, edc, dct, dtt, wt, etot


def _ssd_direction(xbc_ref, scalars, st_ref, fwd):
    L = SSM_CHUNK
    dcum, edc, dct, dtt, wt, etot = scalars
    ri = lax.broadcasted_iota(jnp.int32, (L, L), 0)
    ci = lax.broadcasted_iota(jnp.int32, (L, L), 1)
    keep = (ri >= ci) if fwd else (ri <= ci)
    left = lax.broadcasted_iota(jnp.int32, (L, LANE), 1) < SSM_HEAD_DIM
    ys, xss = [], []
    for g in range(SSM_GROUPS):
        c0 = g * XBC_GROUP_W
        xs = xbc_ref[:, c0:c0 + GROUP_W]
        bm = xbc_ref[:, c0 + GROUP_W:c0 + GROUP_W + SSM_STATE]
        cm = xbc_ref[:, c0 + GROUP_W + SSM_STATE:c0 + XBC_GROUP_W]
        cb = lax.dot_general(cm, bm, (((1,), (1,)), ((), ())), preferred_element_type=F32)
        cm32 = cm.astype(F32)
        bt32 = bm.astype(F32).T
        st = st_ref[:, g * GROUP_W:(g + 1) * GROUP_W]
        st16 = st.astype(BF16)
        snew = []
        for jp in range(HEADS_PER_GROUP // 2):
            xpair = xs[:, jp * LANE:(jp + 1) * LANE]
            rhs = jnp.concatenate([xpair, st16[:, jp * LANE:(jp + 1) * LANE]], axis=0)
            yp, sp = [], []
            for j in (2 * jp, 2 * jp + 1):
                hl = DT_LANE + (0 if fwd else SSM_HEADS) + g * HEADS_PER_GROUP + j
                seg = jnp.where(keep, dcum[:, hl:hl + 1] - dct[hl:hl + 1, :], -jnp.inf)
                gm = cb * jnp.exp(seg) * dtt[hl:hl + 1, :]
                ce = cm32 * edc[:, hl:hl + 1]
                lhs = jnp.concatenate([gm.astype(BF16), ce.astype(BF16)], axis=1)
                yp.append(jnp.dot(lhs, rhs, preferred_element_type=F32))
                btj = (bt32 * wt[hl:hl + 1, :]).astype(BF16)
                sp.append(jnp.dot(btj, xpair, preferred_element_type=F32))
            ys.append(jnp.where(left, yp[0], yp[1]))
            snew.append(jnp.where(left, sp[0], sp[1]))
        st_ref[:, g * GROUP_W:(g + 1) * GROUP_W] = (etot[:, g * GROUP_W:(g + 1) * GROUP_W] * st
                                                    + jnp.concatenate(snew, axis=1))
        xss.append(xs)
    return jnp.concatenate(ys, axis=1), jnp.concatenate(xss, axis=1)


def _ssd_kernel(xf_ref, dtf_ref, xb_ref, dtb_ref, exp_ref, dsk_ref, yf_ref, yb_ref, stf_scr, stb_scr, *,
                ctx_chunks):
    s = pl.program_id(1)

    @pl.when(s == 0)
    def _():
        stf_scr[...] = jnp.zeros_like(stf_scr)
        stb_scr[...] = jnp.zeros_like(stb_scr)

    sc_f = _ssd_scalars(dtf_ref, exp_ref.at[0], True)
    sc_b = _ssd_scalars(dtb_ref, exp_ref.at[1], False)
    yf, xs = _ssd_direction(xf_ref, sc_f, stf_scr, True)
    yb, _ = _ssd_direction(xb_ref, sc_b, stb_scr, False)

    @pl.when(s >= ctx_chunks)
    def _():
        yf_ref[...] = yf + dsk_ref[...] * xs.astype(F32)
        yb_ref[...] = yb


def _ssd(u, dtc, d_skip, n_lat):
    bsz, n_total, _ = u.shape
    L = SSM_CHUNK
    nct, ncl = n_total // L, n_lat // L
    ncc = nct - ncl
    expand = np.zeros((2, LANE, SSM_INNER), np.float32)
    for d in range(2):
        for h in range(SSM_HEADS):
            expand[d, DT_LANE + d * SSM_HEADS + h, h * SSM_HEAD_DIM:(h + 1) * SSM_HEAD_DIM] = 1.0
    dsk = jnp.repeat(d_skip.astype(F32), SSM_HEAD_DIM).reshape(1, SSM_INNER)

    def cf(s):
        return (s + ncl) % nct

    def cbk(s):
        return nct - 1 - s

    def chunk_specs(cmap):
        return [pl.BlockSpec((None, L, SSM_CONV_CH), lambda b, s: (b, cmap(s), 0)),
                pl.BlockSpec((None, L, 2 * LANE), lambda b, s: (b, cmap(s), 0))]

    in_specs = chunk_specs(cf) + chunk_specs(cbk) + [
        pl.BlockSpec((2, LANE, SSM_INNER), lambda b, s: (0, 0, 0)),
        pl.BlockSpec((1, SSM_INNER), lambda b, s: (0, 0))]
    out_specs = [pl.BlockSpec((None, L, SSM_INNER), lambda b, s: (b, jnp.maximum(s - ncc, 0), 0)),
                 pl.BlockSpec((None, L, SSM_INNER), lambda b, s: (b, jnp.minimum(cbk(s), ncl - 1), 0))]
    return pl.pallas_call(
        functools.partial(_ssd_kernel, ctx_chunks=ncc),
        grid=(bsz, nct),
        in_specs=in_specs,
        out_specs=out_specs,
        out_shape=[jax.ShapeDtypeStruct((bsz, n_lat, SSM_INNER), F32)] * 2,
        scratch_shapes=[pltpu.VMEM((SSM_STATE, SSM_INNER), F32), pltpu.VMEM((SSM_STATE, SSM_INNER), F32)],
        compiler_params=_cparams(("parallel", "arbitrary"), 32),
        name="ssd",
    )(u, dtc, u, dtc, jnp.asarray(expand, BF16), dsk)


def _rope(blk, cos_ref, s1_ref, s2_ref):
    return (blk * cos_ref[...] + pltpu.roll(blk, LANE - ROPE_HALF, 1) * s1_ref[...]
            + pltpu.roll(blk, ROPE_HALF, 1) * s2_ref[...])


def _qproj_kernel(qc_ref, g_ref, w_ref, cos_ref, s1_ref, s2_ref, q_ref):
    y = (_rms(qc_ref[...].astype(F32)) * g_ref[...]).astype(BF16)
    for h in range(MLA_HEADS):
        qh = jnp.dot(y, w_ref[:, h * HEAD_PAD:(h + 1) * HEAD_PAD], preferred_element_type=F32) * Q_SCALE
        q_ref[:, h * HEAD_PAD:h * HEAD_PAD + LANE] = qh[:, :LANE].astype(BF16)
        q_ref[:, h * HEAD_PAD + LANE:(h + 1) * HEAD_PAD] = _rope(qh[:, LANE:], cos_ref, s1_ref, s2_ref).astype(BF16)


def _kvproj_kernel(kvc_ref, kr_ref, g_ref, w_ref, cos_ref, s1_ref, s2_ref, k_ref, vt_ref):
    y = (_rms(kvc_ref[...].astype(F32)) * g_ref[...]).astype(BF16)
    kr = _rope(kr_ref[...], cos_ref, s1_ref, s2_ref).astype(BF16)
    for h in range(MLA_HEADS):
        kv = jnp.dot(y, w_ref[:, h * HEAD_PAD:(h + 1) * HEAD_PAD], preferred_element_type=F32)
        k_ref[:, h * HEAD_PAD:h * HEAD_PAD + LANE] = kv[:, :QK_NOPE].astype(BF16)
        k_ref[:, h * HEAD_PAD + LANE:(h + 1) * HEAD_PAD] = kr
        vt_ref[h * V_HEAD:(h + 1) * V_HEAD, :] = kv[:, QK_NOPE:].T.astype(BF16)


def _rope_tables(n_lat, n_total):
    rows = n_lat // GRID_W
    row = np.repeat(np.arange(rows, dtype=np.float32), GRID_W)
    col = np.tile(np.arange(GRID_W, dtype=np.float32), rows)
    inv = jnp.asarray(ROPE_THETA, F32) ** (-jnp.arange(ROPE_AXIS_FREQS, dtype=F32) / ROPE_AXIS_FREQS)
    ang = jnp.concatenate([jnp.asarray(row)[:, None] * inv, jnp.asarray(col)[:, None] * inv], axis=-1)
    ang = jnp.pad(ang, ((0, n_total - n_lat), (0, 0)))
    cos, sin = jnp.cos(ang), jnp.sin(ang)
    z32 = jnp.zeros_like(sin)
    z64 = jnp.zeros((n_total, LANE - QK_ROPE), F32)
    return (jnp.concatenate([cos, cos, z64], axis=1),
            jnp.concatenate([-sin, z32, z64], axis=1),
            jnp.concatenate([z32, sin, z64], axis=1))


def _qproj(p, gain, w, tables, n_lat):
    bsz = p.shape[0]
    tm = _tile(n_lat, 512)
    tab = pl.BlockSpec((tm, LANE), lambda b, i: (i, 0))
    return pl.pallas_call(
        _qproj_kernel,
        grid=(bsz, n_lat // tm),
        in_specs=[pl.BlockSpec((None, tm, Q_LORA), lambda b, i: (b, i, P_QC // Q_LORA)),
                  pl.BlockSpec((1, Q_LORA), lambda b, i: (0, 0)),
                  pl.BlockSpec((Q_LORA, MLA_HEADS * HEAD_PAD), lambda b, i: (0, 0)),
                  tab, tab, tab],
        out_specs=pl.BlockSpec((None, tm, MLA_HEADS * HEAD_PAD), lambda b, i: (b, i, 0)),
        out_shape=jax.ShapeDtypeStruct((bsz, n_lat, MLA_HEADS * HEAD_PAD), BF16),
        compiler_params=_cparams(("parallel", "parallel"), 40),
        name="qproj",
    )(p, gain, w, *tables)


def _kvproj(p, kd, gain, w, tables):
    bsz, n_total, _ = p.shape
    tm = _tile(n_total, 256)
    tab = pl.BlockSpec((tm, LANE), lambda b, i: (i, 0))
    return pl.pallas_call(
        _kvproj_kernel,
        grid=(bsz, n_total // tm),
        in_specs=[pl.BlockSpec((None, tm, KV_LORA), lambda b, i: (b, i, P_KVC // KV_LORA)),
                  pl.BlockSpec((None, tm, LANE), lambda b, i: (b, i, 0)),
                  pl.BlockSpec((1, KV_LORA), lambda b, i: (0, 0)),
                  pl.BlockSpec((KV_LORA, MLA_HEADS * HEAD_PAD), lambda b, i: (0, 0)),
                  tab, tab, tab],
        out_specs=[pl.BlockSpec((None, tm, MLA_HEADS * HEAD_PAD), lambda b, i: (b, i, 0)),
                   pl.BlockSpec((None, MLA_OUT, tm), lambda b, i: (b, 0, i))],
        out_shape=[jax.ShapeDtypeStruct((bsz, n_total, MLA_HEADS * HEAD_PAD), BF16),
                   jax.ShapeDtypeStruct((bsz, MLA_OUT, n_total), BF16)],
        compiler_params=_cparams(("parallel", "parallel"), 40),
        name="kvproj",
    )(p, kd, gain, w, *tables)


def _attn_kernel(q_ref, k_ref, vt_ref, o_ref, *, sub, qsub):
    nk = k_ref.shape[0]
    bounds = []
    for k0 in range(0, nk, sub):
        k1 = nk if nk - k0 < 2 * sub else k0 + sub
        bounds.append((k0, k1))
        if k1 == nk:
            break

    def weighted(b, p):
        vt = jnp.concatenate([vt_ref[:, b[0]:b[1]], jnp.ones((HALO16, b[1] - b[0]), BF16)], axis=0)
        return jnp.dot(vt, p.astype(BF16), preferred_element_type=F32)

    for c0 in range(0, q_ref.shape[0], qsub):
        q = q_ref[c0:c0 + qsub, :]

        def scores(b):
            return lax.dot_general(k_ref[b[0]:b[1], :], q, (((1,), (1,)), ((), ())), preferred_element_type=F32)

        m = acc = None
        s_next = scores(bounds[0])
        for bi, b in enumerate(bounds):
            s = s_next
            if bi + 1 < len(bounds):
                s_next = scores(bounds[bi + 1])
            ms = jnp.max(s, axis=0, keepdims=True)
            if m is None:
                m = ms
                acc = weighted(b, jnp.exp2(s - m))
            else:
                mn = jnp.maximum(m, ms)
                acc = jnp.exp2(m - mn) * acc + weighted(b, jnp.exp2(s - mn))
                m = mn
        o_ref[c0:c0 + qsub, :] = (acc[:V_HEAD] / acc[V_HEAD:V_HEAD + 1]).T.astype(BF16)


def _attention(q, k, vt):
    bsz, n_lat, _ = q.shape
    n_total = k.shape[1]
    tq = _tile(n_lat, 1024)
    return pl.pallas_call(
        functools.partial(_attn_kernel, sub=1024, qsub=tq),
        grid=(bsz, MLA_HEADS, n_lat // tq),
        in_specs=[pl.BlockSpec((None, tq, HEAD_PAD), lambda b, h, i: (b, i, h)),
                  pl.BlockSpec((None, n_total, HEAD_PAD), lambda b, h, i: (b, 0, h)),
                  pl.BlockSpec((None, V_HEAD, n_total), lambda b, h, i: (b, h, 0))],
        out_specs=pl.BlockSpec((None, tq, V_HEAD), lambda b, h, i: (b, i, h)),
        out_shape=jax.ShapeDtypeStruct((bsz, n_lat, MLA_OUT), BF16),
        compiler_params=_cparams(("parallel", "parallel", "parallel"), 48),
        name="attn",
    )(q, k, vt)


def _outproj_kernel(h_ref, yf_ref, yb_ref, z_ref, o_ref, gain_ref, gate_ref, w_ref, out_ref):
    gw = SSM_INNER // SSM_GROUPS
    mix = jnp.dot(o_ref[...], w_ref[SSM_INNER:, :], preferred_element_type=F32)
    for gi in range(SSM_GROUPS):
        sl = slice(gi * gw, (gi + 1) * gw)
        g = (yf_ref[:, sl] + yb_ref[:, sl]) * jax.nn.silu(z_ref[:, sl].astype(F32))
        lhs = (_rms(g) * gain_ref[:, sl]).astype(BF16)
        mix = mix + jnp.dot(lhs, w_ref[sl, :], preferred_element_type=F32)
    out_ref[...] = h_ref[...] + gate_ref[...] * mix


def _outproj(h, yf, yb, p, o, gain, mods, layer, w):
    bsz, n, d = h.shape
    kdim = w.shape[0]
    tm = _tile(n, 256)
    return pl.pallas_call(
        _outproj_kernel,
        grid=(bsz, n // tm),
        in_specs=[pl.BlockSpec((None, tm, d), lambda b, i: (b, i, 0)),
                  pl.BlockSpec((None, tm, SSM_INNER), lambda b, i: (b, i, 0)),
                  pl.BlockSpec((None, tm, SSM_INNER), lambda b, i: (b, i, 0)),
                  pl.BlockSpec((None, tm, SSM_INNER), lambda b, i: (b, i, P_Z // SSM_INNER)),
                  pl.BlockSpec((None, tm, MLA_OUT), lambda b, i: (b, i, 0)),
                  pl.BlockSpec((1, SSM_INNER), lambda b, i: (0, 0)),
                  _mod_spec(d, layer, 5, None),
                  pl.BlockSpec((kdim, d), lambda b, i: (0, 0), pipeline_mode=pl.Buffered(1))],
        out_specs=pl.BlockSpec((None, tm, d), lambda b, i: (b, i, 0)),
        out_shape=jax.ShapeDtypeStruct((bsz, n, d), F32),
        compiler_params=_cparams(("parallel", "parallel"), 56),
        name="outproj",
    )(h, yf, yb, p, o, gain, mods, w)


def _pool_kernel(cur_ref, prev_ref, next_ref, sh_ref, sc_ref, gate_ref, w_ref, ps_ref, out_ref, *, n):
    i = pl.program_id(1)
    tm, d = cur_ref.shape
    pg = d // len(POOL_WINDOWS)
    cur = cur_ref[...]
    xe = jnp.concatenate([prev_ref[...], cur, next_ref[...]], axis=0)
    xe = _rms(xe) * (1.0 + sc_ref[...]) + sh_ref[...]
    pos = i * tm - HALO + lax.broadcasted_iota(jnp.int32, (tm + 2 * HALO, 1), 0)
    xe = jnp.where(jnp.logical_and(pos >= 0, pos < n), xe, 0.0)
    t = i * tm + lax.broadcasted_iota(jnp.int32, (tm, 1), 0)
    for gi, win in enumerate(POOL_WINDOWS):
        x = xe[:, gi * pg:(gi + 1) * pg]
        acc = x[0:tm + 2 * HALO - 1] + x[1:tm + 2 * HALO]
        span, off = 2, 1
        while span < win:
            rows = acc.shape[0] - span
            acc = acc[0:rows] + acc[span:span + rows]
            off += span // 2
            span *= 2
        wsum = acc[HALO - off:HALO - off + tm]
        cnt = (jnp.minimum(t + win // 2, n) - jnp.maximum(t - win // 2, 0)).astype(F32)
        pooled = (wsum / cnt - xe[HALO:HALO + tm, gi * pg:(gi + 1) * pg]).astype(BF16)
        mix = jnp.dot(pooled, w_ref[gi], preferred_element_type=F32) * ps_ref[:, gi * pg:(gi + 1) * pg]
        out_ref[:, gi * pg:(gi + 1) * pg] = cur[:, gi * pg:(gi + 1) * pg] + gate_ref[:, gi * pg:(gi + 1) * pg] * mix


def _pool(h, mods, layer, w, scale):
    bsz, n, d = h.shape
    tm = _tile(n, 256)
    pg = d // len(POOL_WINDOWS)
    hb = tm // HALO
    last_hb = n // HALO - 1
    return pl.pallas_call(
        functools.partial(_pool_kernel, n=n),
        grid=(bsz, n // tm),
        in_specs=[pl.BlockSpec((None, tm, d), lambda b, i: (b, i, 0)),
                  pl.BlockSpec((None, HALO, d), lambda b, i: (b, jnp.maximum(i * hb - 1, 0), 0)),
                  pl.BlockSpec((None, HALO, d), lambda b, i: (b, jnp.minimum((i + 1) * hb, last_hb), 0)),
                  _mod_spec(d, layer, 3, None), _mod_spec(d, layer, 4, None), _mod_spec(d, layer, 5, None),
                  pl.BlockSpec((len(POOL_WINDOWS), pg, pg), lambda b, i: (0, 0, 0)),
                  pl.BlockSpec((1, d), lambda b, i: (0, 0))],
        out_specs=pl.BlockSpec((None, tm, d), lambda b, i: (b, i, 0)),
        out_shape=jax.ShapeDtypeStruct((bsz, n, d), F32),
        compiler_params=_cparams(("parallel", "parallel"), 40),
        name="pool",
    )(h, h, h, mods, mods, mods, w, scale.reshape(1, d))


def _regroup_xbc(a):
    parts = []
    for g in range(SSM_GROUPS):
        parts += [a[..., g * GROUP_W:(g + 1) * GROUP_W],
                  a[..., SSM_INNER + g * SSM_STATE:SSM_INNER + (g + 1) * SSM_STATE],
                  a[..., SSM_INNER + SSM_GN + g * SSM_STATE:SSM_INNER + SSM_GN + (g + 1) * SSM_STATE]]
    return jnp.concatenate(parts, axis=-1)


def _prep_w_in(w_in):
    q_c, kv_c, k_r, z, xbc, dtr = jnp.split(
        w_in, np.cumsum((Q_LORA, KV_LORA, QK_ROPE, SSM_INNER, SSM_CONV_CH))[:].tolist(), axis=-1)
    w = jnp.concatenate([z, _regroup_xbc(xbc), q_c, kv_c], axis=-1)
    return w.astype(BF16), jnp.concatenate([k_r, dtr], axis=-1).astype(BF16)


def _prep_w_uq(w_uq):
    w = w_uq.reshape(Q_LORA, MLA_HEADS, QK_NOPE + QK_ROPE)
    w = jnp.pad(w, ((0, 0), (0, 0), (0, HEAD_PAD - QK_NOPE - QK_ROPE)))
    return w.reshape(Q_LORA, MLA_HEADS * HEAD_PAD).astype(BF16)


def kernel(x, c, ctx, c_ctx, mod_w, mod_b, ffn_w_gate, ffn_w_up, ffn_w_down, w_in, conv_w, conv_b, dt_bias, a_log,
           d_skip, ssm_norm, q_norm, kv_norm, w_uq, w_ukv, w_out, pool_w, pool_scale, final_norm):
    bsz, n, d = x.shape
    n_ctx = ctx.shape[1]
    n_total = n + n_ctx
    depth = mod_w.shape[0]
    assert bsz + 1 <= MOD_ROWS

    cond = jnp.concatenate([c, c_ctx[None, :], jnp.zeros((MOD_ROWS - bsz - 1, d), F32)], axis=0)
    mods = _adaln(cond, mod_w, mod_b).reshape(depth * MOD_ROWS * N_MOD, 1, d)
    wg, wu, wd = ffn_w_gate.astype(BF16), ffn_w_up.astype(BF16), ffn_w_down.astype(BF16)

    h, hc = x, ctx
    for l in range(depth):
        j = l // 2
        last = l == depth - 1
        if l % 2 == 0:
            h = _ffn(h, mods, l, 0, None, wg, wu, wd)
            hc = _ffn(hc.reshape(1, bsz * n_ctx, d), mods, l, 0, bsz, wg, wu, wd).reshape(bsz, n_ctx, d)
            w_in_p, w_kd = _prep_w_in(w_in[j])
            pk = _inproj(h, mods, l, None, w_in_p, w_kd, n_total, 0)
            p, kd = _inproj(hc, mods, l, bsz, w_in_p, w_kd, n_total, n, prev=pk)
            conv_w8 = jnp.pad(_regroup_xbc(conv_w[j]), ((0, HALO - SSM_CONV), (0, 0)))
            dtb = jnp.pad(dt_bias[j].reshape(1, 2 * SSM_HEADS).astype(F32), ((0, 0), (DT_LANE, 0)))
            u, dtc = _conv(p, kd, conv_w8, _regroup_xbc(conv_b[j]).reshape(1, SSM_CONV_CH), dtb, a_log[j], n)
            yf, yb = _ssd(u, dtc, d_skip[j], n)
            tables = _rope_tables(n, n_total)
            q = _qproj(p, q_norm[j].reshape(1, Q_LORA), _prep_w_uq(w_uq[j]), tables, n)
            k, vt = _kvproj(p, kd, kv_norm[j].reshape(1, KV_LORA), w_ukv[j].astype(BF16), tables)
            o = _attention(q, k, vt)
            h = _outproj(h, yf, yb, p, o, ssm_norm[j].reshape(1, SSM_INNER), mods, l, w_out[j].astype(BF16))
        else:
            h = _ffn(h, mods, l, 0, None, wg, wu, wd)
            h = _pool(h, mods, l, pool_w[j].astype(BF16), pool_scale[j])
        h = _ffn(h, mods, l, 1, None, wg, wu, wd, final_gain=final_norm if last else None)
    return h
```

```python
import functools

import jax
import jax.numpy as jnp
import numpy as np
from jax import lax
from jax.experimental import pallas as pl
from jax.experimental.pallas import tpu as pltpu

F32 = jnp.float32
BF16 = jnp.bfloat16

EPS = 1e-6
N_MOD = 9
MOD_ROWS = 8
GRID_W = 64

SSM_HEADS = 32
SSM_HEAD_DIM = 64
SSM_INNER = SSM_HEADS * SSM_HEAD_DIM
SSM_GROUPS = 4
SSM_STATE = 128
SSM_CONV = 5
SSM_CHUNK = 128
SSM_GN = SSM_GROUPS * SSM_STATE
SSM_CONV_CH = SSM_INNER + 2 * SSM_GN
HEADS_PER_GROUP = SSM_HEADS // SSM_GROUPS
GROUP_W = HEADS_PER_GROUP * SSM_HEAD_DIM
XBC_GROUP_W = GROUP_W + 2 * SSM_STATE

MLA_HEADS = 16
Q_LORA = 512
KV_LORA = 512
QK_NOPE = 128
QK_ROPE = 64
V_HEAD = 128
MLA_OUT = MLA_HEADS * V_HEAD
MLA_SCALE = (QK_NOPE + QK_ROPE) ** -0.5
Q_SCALE = MLA_SCALE * 1.4426950408889634
ROPE_THETA = 10000.0
ROPE_HALF = QK_ROPE // 2
ROPE_AXIS_FREQS = QK_ROPE // 4
HEAD_PAD = 256

LANE = 128
HALO = 8
HALO16 = 16

P_Z = 0
P_XBC = P_Z + SSM_INNER
P_QC = P_XBC + SSM_CONV_CH
P_KVC = P_QC + Q_LORA
DT_LANE = QK_ROPE

POOL_WINDOWS = (2, 4, 8, 16)


def _tile(dim, pref):
    t = min(dim, pref)
    while dim % t:
        t //= 2
    return t


def _cparams(sem, vmem_mb):
    return pltpu.CompilerParams(dimension_semantics=sem, vmem_limit_bytes=vmem_mb << 20)


def _rms(x):
    return x * lax.rsqrt(jnp.mean(x * x, axis=-1, keepdims=True) + EPS)


def _mod_spec(d, layer, k, row):
    if row is None:
        return pl.BlockSpec((None, 1, d), lambda b, *_: ((layer * MOD_ROWS + b) * N_MOD + k, 0, 0))
    return pl.BlockSpec((None, 1, d), lambda b, *_: ((layer * MOD_ROWS + row) * N_MOD + k, 0, 0))


def _adaln_kernel(c_ref, w_ref, b_ref, o_ref):
    a = jax.nn.silu(c_ref[...]).astype(BF16)
    o_ref[...] = jnp.dot(a, w_ref[...].astype(BF16), preferred_element_type=F32) + b_ref[...]


def _adaln(cond, mod_w, mod_b):
    depth, d, nd = mod_w.shape
    tn = _tile(nd, 1024)
    return pl.pallas_call(
        _adaln_kernel,
        grid=(depth, nd // tn),
        in_specs=[pl.BlockSpec((MOD_ROWS, d), lambda l, j: (0, 0)),
                  pl.BlockSpec((None, d, tn), lambda l, j: (l, 0, j)),
                  pl.BlockSpec((None, 1, tn), lambda l, j: (l, 0, j))],
        out_specs=pl.BlockSpec((None, MOD_ROWS, tn), lambda l, j: (l, 0, j)),
        out_shape=jax.ShapeDtypeStruct((depth, MOD_ROWS, nd), F32),
        compiler_params=_cparams(("parallel", "parallel"), 40),
        name="adaln",
    )(cond, mod_w, mod_b.reshape(depth, 1, nd))


def _cast_kernel(a_ref, b_ref, oa_ref, ob_ref):
    oa_ref[...] = a_ref[...].astype(BF16)
    ob_ref[...] = b_ref[...].astype(BF16)


def _chunk_major_bf16(wg, wu):
    depth, two, d, f = wg.shape
    tf = _tile(f, 512)
    src = pl.BlockSpec((None, None, d, tf), lambda l, s, j: (l, s, 0, j))
    dst = pl.BlockSpec((None, None, None, d, tf), lambda l, s, j: (l, s, j, 0, 0))
    shape = jax.ShapeDtypeStruct((depth, two, f // tf, d, tf), BF16)
    return pl.pallas_call(
        _cast_kernel,
        grid=(depth, two, f // tf),
        in_specs=[src, src],
        out_specs=[dst, dst],
        out_shape=[shape, shape],
        compiler_params=_cparams(("parallel", "parallel", "parallel"), 40),
        name="wcast",
    )(wg, wu)


def _ffn_kernel(*refs, final_norm):
    if final_norm:
        h_ref, sh_ref, sc_ref, gt_ref, wg_ref, wu_ref, wd_ref, fn_ref, o_ref, u_scr = refs
    else:
        h_ref, sh_ref, sc_ref, gt_ref, wg_ref, wu_ref, wd_ref, o_ref, u_scr = refs
    j = pl.program_id(2)

    @pl.when(j == 0)
    def _():
        u = _rms(h_ref[...]) * (1.0 + sc_ref[...]) + sh_ref[...]
        u_scr[...] = u.astype(BF16)
        o_ref[...] = jnp.zeros_like(o_ref)

    u = u_scr[...]
    g = jnp.dot(u, wg_ref[...], preferred_element_type=F32)
    up = jnp.dot(u, wu_ref[...], preferred_element_type=F32)
    a = (jax.nn.silu(g) * up).astype(BF16)
    o_ref[...] += jnp.dot(a, wd_ref[...], preferred_element_type=F32)

    @pl.when(j == pl.num_programs(2) - 1)
    def _():
        out = h_ref[...] + (0.5 * gt_ref[...]) * o_ref[...]
        if final_norm:
            out = _rms(out) * fn_ref[...]
        o_ref[...] = out


def _ffn(h, mods, layer, half, row, wg, wu, wd, final_gain=None):
    bsz, n, d = h.shape
    nj, tf = wg.shape[2], wg.shape[-1]
    tm = _tile(n, 512)
    k0 = 6 * half
    in_specs = [pl.BlockSpec((None, tm, d), lambda b, i, j: (b, i, 0)),
                _mod_spec(d, layer, k0, row), _mod_spec(d, layer, k0 + 1, row), _mod_spec(d, layer, k0 + 2, row),
                pl.BlockSpec((None, None, None, d, tf), lambda b, i, j: (layer, half, j, 0, 0)),
                pl.BlockSpec((None, None, None, d, tf), lambda b, i, j: (layer, half, j, 0, 0)),
                pl.BlockSpec((None, None, tf, d), lambda b, i, j: (layer, half, j, 0))]
    args = [h, mods, mods, mods, wg, wu, wd]
    if final_gain is not None:
        in_specs.append(pl.BlockSpec((1, d), lambda b, i, j: (0, 0)))
        args.append(final_gain.reshape(1, d))
    return pl.pallas_call(
        functools.partial(_ffn_kernel, final_norm=final_gain is not None),
        grid=(bsz, n // tm, nj),
        in_specs=in_specs,
        out_specs=pl.BlockSpec((None, tm, d), lambda b, i, j: (b, i, 0)),
        out_shape=jax.ShapeDtypeStruct((bsz, n, d), F32),
        scratch_shapes=[pltpu.VMEM((tm, d), BF16)],
        compiler_params=_cparams(("parallel", "parallel", "arbitrary"), 48),
        name="ffn",
    )(*args)


def _inproj_kernel(h_ref, sh_ref, sc_ref, w_ref, wkd_ref, *rest):
    o_ref, kd_ref, u_scr = rest[-3:]

    @pl.when(pl.program_id(2) == 0)
    def _():
        u = (_rms(h_ref[...]) * (1.0 + sc_ref[...]) + sh_ref[...]).astype(BF16)
        u_scr[...] = u
        kd_ref[...] = jnp.dot(u, wkd_ref[...], preferred_element_type=F32)

    o_ref[...] = jnp.dot(u_scr[...], w_ref[...], preferred_element_type=F32).astype(BF16)


def _inproj(h, mods, layer, row, w, wkd, n_total, row_off, prev=None):
    bsz, n, d = h.shape
    cols = w.shape[1]
    tm, tn = _tile(n, 1024), _tile(cols, 1536)
    off = row_off // tm
    in_specs = [pl.BlockSpec((None, tm, d), lambda b, i, j: (b, i, 0)),
                _mod_spec(d, layer, 3, row), _mod_spec(d, layer, 4, row),
                pl.BlockSpec((d, tn), lambda b, i, j: (0, j)),
                pl.BlockSpec((d, LANE), lambda b, i, j: (0, 0))]
    args = [h, mods, mods, w, wkd]
    aliases = {}
    if prev is not None:
        in_specs += [pl.BlockSpec(memory_space=pl.ANY)] * 2
        args += list(prev)
        aliases = {5: 0, 6: 1}
    return pl.pallas_call(
        _inproj_kernel,
        grid=(bsz, n // tm, cols // tn),
        in_specs=in_specs,
        out_specs=[pl.BlockSpec((None, tm, tn), lambda b, i, j: (b, off + i, j)),
                   pl.BlockSpec((None, tm, LANE), lambda b, i, j: (b, off + i, 0))],
        out_shape=[jax.ShapeDtypeStruct((bsz, n_total, cols), BF16),
                   jax.ShapeDtypeStruct((bsz, n_total, LANE), F32)],
        scratch_shapes=[pltpu.VMEM((tm, d), BF16)],
        input_output_aliases=aliases,
        compiler_params=_cparams(("parallel", "parallel", "arbitrary"), 48),
        name="inproj",
    )(*args)


def _exact_dot(lhs, rhs, lhs_split):
    val = lhs if lhs_split else rhs
    acc = None
    for i in range(3):
        piece = val.astype(BF16)
        if lhs_split:
            d = jnp.dot(piece, rhs, preferred_element_type=F32)
        else:
            d = jnp.dot(lhs, piece, preferred_element_type=F32)
        acc = d if acc is None else acc + d
        if i < 2:
            val = val - piece.astype(F32)
    return acc


def _conv_kernel(cur_ref, prev_ref, next_ref, dtr_ref, w_ref, b_ref, dtb_ref, aneg_ref, tri_ref,
                 u_ref, dtc_ref, *, lat_tiles):
    t = pl.program_id(1)
    tc = cur_ref.shape[0]
    first = jnp.logical_or(t == 0, t == lat_tiles)
    last = jnp.logical_or(t == lat_tiles - 1, t == pl.num_programs(1) - 1)
    prev = jnp.where(first, 0.0, prev_ref[...].astype(F32))
    nxt = jnp.where(last, 0.0, next_ref[...].astype(F32))
    xe = jnp.concatenate([prev, cur_ref[...].astype(F32), nxt], axis=0)
    w = w_ref[...]
    ext = tc + 2 * HALO16
    acc = b_ref[...] + w[2:3, :] * xe[HALO16:HALO16 + tc, :]
    for k in (0, 1, 3, 4):
        acc = acc + w[k:k + 1, :] * pltpu.roll(xe, (2 - k) % ext, 0)[HALO16:HALO16 + tc, :]
    u_ref[...] = jax.nn.silu(acc).astype(BF16)

    @pl.when(pl.program_id(2) == 0)
    def _():
        dt = jax.nn.softplus(dtr_ref[...] + dtb_ref[...])
        dtc_ref[:, :LANE] = dt
        a = dt * aneg_ref[...]
        fwd_lane = lax.broadcasted_iota(jnp.int32, (SSM_CHUNK, LANE), 1) < DT_LANE + SSM_HEADS
        for r in range(tc // SSM_CHUNK):
            rows = slice(r * SSM_CHUNK, (r + 1) * SSM_CHUNK)
            pre = _exact_dot(tri_ref[0], a[rows], lhs_split=False)
            suf = _exact_dot(tri_ref[1], a[rows], lhs_split=False)
            dtc_ref[rows, LANE:] = jnp.where(fwd_lane, pre, suf)


def _conv(p, kd, conv_w8, conv_b, dt_bias_row, a_log, n_lat):
    bsz, n_total, _ = p.shape
    tc = 256
    cw = 1024
    L = SSM_CHUNK
    aneg = jnp.pad(-jnp.exp(a_log.astype(F32)).reshape(1, 2 * SSM_HEADS), ((0, 0), (DT_LANE, 0)))
    tri = jnp.asarray(np.stack([np.tril(np.ones((L, L), np.float32)), np.triu(np.ones((L, L), np.float32))]), BF16)
    assert n_lat % tc == 0 and n_total % tc == 0 and SSM_CONV_CH % cw == 0 and P_XBC % cw == 0
    c0 = P_XBC // cw
    hb = tc // HALO16
    last_hb = n_total // HALO16 - 1
    return pl.pallas_call(
        functools.partial(_conv_kernel, lat_tiles=n_lat // tc),
        grid=(bsz, n_total // tc, SSM_CONV_CH // cw),
        in_specs=[pl.BlockSpec((None, tc, cw), lambda b, t, c: (b, t, c0 + c)),
                  pl.BlockSpec((None, HALO16, cw), lambda b, t, c: (b, jnp.maximum(t * hb - 1, 0), c0 + c)),
                  pl.BlockSpec((None, HALO16, cw), lambda b, t, c: (b, jnp.minimum((t + 1) * hb, last_hb), c0 + c)),
                  pl.BlockSpec((None, tc, LANE), lambda b, t, c: (b, t, 0)),
                  pl.BlockSpec((HALO, cw), lambda b, t, c: (0, c)),
                  pl.BlockSpec((1, cw), lambda b, t, c: (0, c)),
                  pl.BlockSpec((1, LANE), lambda b, t, c: (0, 0)),
                  pl.BlockSpec((1, LANE), lambda b, t, c: (0, 0)),
                  pl.BlockSpec((2, L, L), lambda b, t, c: (0, 0, 0))],
        out_specs=[pl.BlockSpec((None, tc, cw), lambda b, t, c: (b, t, c)),
                   pl.BlockSpec((None, tc, 2 * LANE), lambda b, t, c: (b, t, 0))],
        out_shape=[jax.ShapeDtypeStruct((bsz, n_total, SSM_CONV_CH), BF16),
                   jax.ShapeDtypeStruct((bsz, n_total, 2 * LANE), F32)],
        compiler_params=_cparams(("parallel", "parallel", "arbitrary"), 32),
        name="conv",
    )(p, p, p, kd, conv_w8, conv_b, dt_bias_row, aneg, tri)


def _ssd_scalars(dtc_ref, exp_ref, fwd):
    L = SSM_CHUNK
    dcum = dtc_ref[:, LANE:]
    edc = jnp.exp(dcum)
    dct = dcum.T
    dtt = dtc_ref[:, :LANE].T
    tot = L - 1 if fwd else 0
    wt = dtt * jnp.exp(dct[:, tot:tot + 1] - dct)
    etot = jnp.broadcast_to(edc[tot:tot + 1, :], (HALO, LANE))
    etot = _exact_dot(etot, exp_ref[...], lhs_split=True)[0:1, :]
    return dcum, edc, dct, dtt, wt, etot


def _ssd_direction(xbc_ref, scalars, st_ref, fwd):
    L = SSM_CHUNK
    dcum, edc, dct, dtt, wt, etot = scalars
    ri = lax.broadcasted_iota(jnp.int32, (L, L), 0)
    ci = lax.broadcasted_iota(jnp.int32, (L, L), 1)
    keep = (ri >= ci) if fwd else (ri <= ci)
    left = lax.broadcasted_iota(jnp.int32, (L, LANE), 1) < SSM_HEAD_DIM
    ys, xss = [], []
    for g in range(SSM_GROUPS):
        c0 = g * XBC_GROUP_W
        xs = xbc_ref[:, c0:c0 + GROUP_W]
        bm = xbc_ref[:, c0 + GROUP_W:c0 + GROUP_W + SSM_STATE]
        cm = xbc_ref[:, c0 + GROUP_W + SSM_STATE:c0 + XBC_GROUP_W]
        cb = lax.dot_general(cm, bm, (((1,), (1,)), ((), ())), preferred_element_type=F32)
        cm32 = cm.astype(F32)
        bt32 = bm.astype(F32).T
        st = st_ref[:, g * GROUP_W:(g + 1) * GROUP_W]
        st16 = st.astype(BF16)
        snew = []
        for jp in range(HEADS_PER_GROUP // 2):
            xpair = xs[:, jp * LANE:(jp + 1) * LANE]
            rhs = jnp.concatenate([xpair, st16[:, jp * LANE:(jp + 1) * LANE]], axis=0)
            yp, sp = [], []
            for j in (2 * jp, 2 * jp + 1):
                hl = DT_LANE + (0 if fwd else SSM_HEADS) + g * HEADS_PER_GROUP + j
                seg = jnp.where(keep, dcum[:, hl:hl + 1] - dct[hl:hl + 1, :], -jnp.inf)
                gm = cb * jnp.exp(seg) * dtt[hl:hl + 1, :]
                ce = cm32 * edc[:, hl:hl + 1]
                lhs = jnp.concatenate([gm.astype(BF16), ce.astype(BF16)], axis=1)
                yp.append(jnp.dot(lhs, rhs, preferred_element_type=F32))
                btj = (bt32 * wt[hl:hl + 1, :]).astype(BF16)
                sp.append(jnp.dot(btj, xpair, preferred_element_type=F32))
            ys.append(jnp.where(left, yp[0], yp[1]))
            snew.append(jnp.where(left, sp[0], sp[1]))
        st_ref[:, g * GROUP_W:(g + 1) * GROUP_W] = (etot[:, g * GROUP_W:(g + 1) * GROUP_W] * st
                                                    + jnp.concatenate(snew, axis=1))
        xss.append(xs)
    return jnp.concatenate(ys, axis=1), jnp.concatenate(xss, axis=1)


def _ssd_kernel(xf_ref, dtf_ref, xb_ref, dtb_ref, exp_ref, dsk_ref, yf_ref, yb_ref, stf_scr, stb_scr, *,
                ctx_chunks):
    s = pl.program_id(1)

    @pl.when(s == 0)
    def _():
        stf_scr[...] = jnp.zeros_like(stf_scr)
        stb_scr[...] = jnp.zeros_like(stb_scr)

    sc_f = _ssd_scalars(dtf_ref, exp_ref.at[0], True)
    sc_b = _ssd_scalars(dtb_ref, exp_ref.at[1], False)
    yf, xs = _ssd_direction(xf_ref, sc_f, stf_scr, True)
    yb, _ = _ssd_direction(xb_ref, sc_b, stb_scr, False)

    @pl.when(s >= ctx_chunks)
    def _():
        yf_ref[...] = yf + dsk_ref[...] * xs.astype(F32)
        yb_ref[...] = yb


def _ssd(u, dtc, d_skip, n_lat):
    bsz, n_total, _ = u.shape
    L = SSM_CHUNK
    nct, ncl = n_total // L, n_lat // L
    ncc = nct - ncl
    expand = np.zeros((2, LANE, SSM_INNER), np.float32)
    for d in range(2):
        for h in range(SSM_HEADS):
            expand[d, DT_LANE + d * SSM_HEADS + h, h * SSM_HEAD_DIM:(h + 1) * SSM_HEAD_DIM] = 1.0
    dsk = jnp.repeat(d_skip.astype(F32), SSM_HEAD_DIM).reshape(1, SSM_INNER)

    def cf(s):
        return (s + ncl) % nct

    def cbk(s):
        return nct - 1 - s

    def chunk_specs(cmap):
        return [pl.BlockSpec((None, L, SSM_CONV_CH), lambda b, s: (b, cmap(s), 0)),
                pl.BlockSpec((None, L, 2 * LANE), lambda b, s: (b, cmap(s), 0))]

    in_specs = chunk_specs(cf) + chunk_specs(cbk) + [
        pl.BlockSpec((2, LANE, SSM_INNER), lambda b, s: (0, 0, 0)),
        pl.BlockSpec((1, SSM_INNER), lambda b, s: (0, 0))]
    out_specs = [pl.BlockSpec((None, L, SSM_INNER), lambda b, s: (b, jnp.maximum(s - ncc, 0), 0)),
                 pl.BlockSpec((None, L, SSM_INNER), lambda b, s: (b, jnp.minimum(cbk(s), ncl - 1), 0))]
    return pl.pallas_call(
        functools.partial(_ssd_kernel, ctx_chunks=ncc),
        grid=(bsz, nct),
        in_specs=in_specs,
        out_specs=out_specs,
        out_shape=[jax.ShapeDtypeStruct((bsz, n_lat, SSM_INNER), F32)] * 2,
        scratch_shapes=[pltpu.VMEM((SSM_STATE, SSM_INNER), F32), pltpu.VMEM((SSM_STATE, SSM_INNER), F32)],
        compiler_params=_cparams(("parallel", "arbitrary"), 32),
        name="ssd",
    )(u, dtc, u, dtc, jnp.asarray(expand, BF16), dsk)


def _rope(blk, cos_ref, s1_ref, s2_ref):
    return (blk * cos_ref[...] + pltpu.roll(blk, LANE - ROPE_HALF, 1) * s1_ref[...]
            + pltpu.roll(blk, ROPE_HALF, 1) * s2_ref[...])


def _qproj_kernel(qc_ref, g_ref, w_ref, cos_ref, s1_ref, s2_ref, q_ref):
    y = (_rms(qc_ref[...].astype(F32)) * g_ref[...]).astype(BF16)
    for h in range(MLA_HEADS):
        qh = jnp.dot(y, w_ref[:, h * HEAD_PAD:(h + 1) * HEAD_PAD], preferred_element_type=F32) * Q_SCALE
        q_ref[:, h * HEAD_PAD:h * HEAD_PAD + LANE] = qh[:, :LANE].astype(BF16)
        q_ref[:, h * HEAD_PAD + LANE:(h + 1) * HEAD_PAD] = _rope(qh[:, LANE:], cos_ref, s1_ref, s2_ref).astype(BF16)


def _kvproj_kernel(kvc_ref, kr_ref, g_ref, w_ref, cos_ref, s1_ref, s2_ref, k_ref, vt_ref):
    y = (_rms(kvc_ref[...].astype(F32)) * g_ref[...]).astype(BF16)
    kr = _rope(kr_ref[...], cos_ref, s1_ref, s2_ref).astype(BF16)
    for h in range(MLA_HEADS):
        kv = jnp.dot(y, w_ref[:, h * HEAD_PAD:(h + 1) * HEAD_PAD], preferred_element_type=F32)
        k_ref[:, h * HEAD_PAD:h * HEAD_PAD + LANE] = kv[:, :QK_NOPE].astype(BF16)
        k_ref[:, h * HEAD_PAD + LANE:(h + 1) * HEAD_PAD] = kr
        vt_ref[h * V_HEAD:(h + 1) * V_HEAD, :] = kv[:, QK_NOPE:].T.astype(BF16)


def _rope_tables(n_lat, n_total):
    rows = n_lat // GRID_W
    row = np.repeat(np.arange(rows, dtype=np.float32), GRID_W)
    col = np.tile(np.arange(GRID_W, dtype=np.float32), rows)
    inv = jnp.asarray(ROPE_THETA, F32) ** (-jnp.arange(ROPE_AXIS_FREQS, dtype=F32) / ROPE_AXIS_FREQS)
    ang = jnp.concatenate([jnp.asarray(row)[:, None] * inv, jnp.asarray(col)[:, None] * inv], axis=-1)
    ang = jnp.pad(ang, ((0, n_total - n_lat), (0, 0)))
    cos, sin = jnp.cos(ang), jnp.sin(ang)
    z32 = jnp.zeros_like(sin)
    z64 = jnp.zeros((n_total, LANE - QK_ROPE), F32)
    return (jnp.concatenate([cos, cos, z64], axis=1),
            jnp.concatenate([-sin, z32, z64], axis=1),
            jnp.concatenate([z32, sin, z64], axis=1))


def _qproj(p, gain, w, tables, n_lat):
    bsz = p.shape[0]
    tm = _tile(n_lat, 512)
    tab = pl.BlockSpec((tm, LANE), lambda b, i: (i, 0))
    return pl.pallas_call(
        _qproj_kernel,
        grid=(bsz, n_lat // tm),
        in_specs=[pl.BlockSpec((None, tm, Q_LORA), lambda b, i: (b, i, P_QC // Q_LORA)),
                  pl.BlockSpec((1, Q_LORA), lambda b, i: (0, 0)),
                  pl.BlockSpec((Q_LORA, MLA_HEADS * HEAD_PAD), lambda b, i: (0, 0)),
                  tab, tab, tab],
        out_specs=pl.BlockSpec((None, tm, MLA_HEADS * HEAD_PAD), lambda b, i: (b, i, 0)),
        out_shape=jax.ShapeDtypeStruct((bsz, n_lat, MLA_HEADS * HEAD_PAD), BF16),
        compiler_params=_cparams(("parallel", "parallel"), 40),
        name="qproj",
    )(p, gain, w, *tables)


def _kvproj(p, kd, gain, w, tables):
    bsz, n_total, _ = p.shape
    tm = _tile(n_total, 256)
    tab = pl.BlockSpec((tm, LANE), lambda b, i: (i, 0))
    return pl.pallas_call(
        _kvproj_kernel,
        grid=(bsz, n_total // tm),
        in_specs=[pl.BlockSpec((None, tm, KV_LORA), lambda b, i: (b, i, P_KVC // KV_LORA)),
                  pl.BlockSpec((None, tm, LANE), lambda b, i: (b, i, 0)),
                  pl.BlockSpec((1, KV_LORA), lambda b, i: (0, 0)),
                  pl.BlockSpec((KV_LORA, MLA_HEADS * HEAD_PAD), lambda b, i: (0, 0)),
                  tab, tab, tab],
        out_specs=[pl.BlockSpec((None, tm, MLA_HEADS * HEAD_PAD), lambda b, i: (b, i, 0)),
                   pl.BlockSpec((None, MLA_OUT, tm), lambda b, i: (b, 0, i))],
        out_shape=[jax.ShapeDtypeStruct((bsz, n_total, MLA_HEADS * HEAD_PAD), BF16),
                   jax.ShapeDtypeStruct((bsz, MLA_OUT, n_total), BF16)],
        compiler_params=_cparams(("parallel", "parallel"), 40),
        name="kvproj",
    )(p, kd, gain, w, *tables)


def _attn_kernel(q_ref, k_ref, vt_ref, o_ref, *, sub, qsub):
    nk = k_ref.shape[0]
    bounds = []
    for k0 in range(0, nk, sub):
        k1 = nk if nk - k0 < 2 * sub else k0 + sub
        bounds.append((k0, k1))
        if k1 == nk:
            break

    def weighted(b, p):
        vt = jnp.concatenate([vt_ref[:, b[0]:b[1]], jnp.ones((HALO16, b[1] - b[0]), BF16)], axis=0)
        return jnp.dot(vt, p.astype(BF16), preferred_element_type=F32)

    for c0 in range(0, q_ref.shape[0], qsub):
        q = q_ref[c0:c0 + qsub, :]

        def scores(b):
            return lax.dot_general(k_ref[b[0]:b[1], :], q, (((1,), (1,)), ((), ())), preferred_element_type=F32)

        m = acc = None
        s_next = scores(bounds[0])
        for bi, b in enumerate(bounds):
            s = s_next
            if bi + 1 < len(bounds):
                s_next = scores(bounds[bi + 1])
            ms = jnp.max(s, axis=0, keepdims=True)
            if m is None:
                m = ms
                acc = weighted(b, jnp.exp2(s - m))
            else:
                mn = jnp.maximum(m, ms)
                acc = jnp.exp2(m - mn) * acc + weighted(b, jnp.exp2(s - mn))
                m = mn
        o_ref[c0:c0 + qsub, :] = (acc[:V_HEAD] / acc[V_HEAD:V_HEAD + 1]).T.astype(BF16)


def _attention(q, k, vt):
    bsz, n_lat, _ = q.shape
    n_total = k.shape[1]
    tq = _tile(n_lat, 1024)
    return pl.pallas_call(
        functools.partial(_attn_kernel, sub=1024, qsub=tq),
        grid=(bsz, MLA_HEADS, n_lat // tq),
        in_specs=[pl.BlockSpec((None, tq, HEAD_PAD), lambda b, h, i: (b, i, h)),
                  pl.BlockSpec((None, n_total, HEAD_PAD), lambda b, h, i: (b, 0, h)),
                  pl.BlockSpec((None, V_HEAD, n_total), lambda b, h, i: (b, h, 0))],
        out_specs=pl.BlockSpec((None, tq, V_HEAD), lambda b, h, i: (b, i, h)),
        out_shape=jax.ShapeDtypeStruct((bsz, n_lat, MLA_OUT), BF16),
        compiler_params=_cparams(("parallel", "parallel", "parallel"), 48),
        name="attn",
    )(q, k, vt)


def _outproj_kernel(h_ref, yf_ref, yb_ref, z_ref, o_ref, gain_ref, gate_ref, w_ref, out_ref):
    gw = SSM_INNER // SSM_GROUPS
    mix = jnp.dot(o_ref[...], w_ref[SSM_INNER:, :], preferred_element_type=F32)
    for gi in range(SSM_GROUPS):
        sl = slice(gi * gw, (gi + 1) * gw)
        g = (yf_ref[:, sl] + yb_ref[:, sl]) * jax.nn.silu(z_ref[:, sl].astype(F32))
        lhs = (_rms(g) * gain_ref[:, sl]).astype(BF16)
        mix = mix + jnp.dot(lhs, w_ref[sl, :], preferred_element_type=F32)
    out_ref[...] = h_ref[...] + gate_ref[...] * mix


def _outproj(h, yf, yb, p, o, gain, mods, layer, w):
    bsz, n, d = h.shape
    kdim = w.shape[0]
    tm = _tile(n, 256)
    return pl.pallas_call(
        _outproj_kernel,
        grid=(bsz, n // tm),
        in_specs=[pl.BlockSpec((None, tm, d), lambda b, i: (b, i, 0)),
                  pl.BlockSpec((None, tm, SSM_INNER), lambda b, i: (b, i, 0)),
                  pl.BlockSpec((None, tm, SSM_INNER), lambda b, i: (b, i, 0)),
                  pl.BlockSpec((None, tm, SSM_INNER), lambda b, i: (b, i, P_Z // SSM_INNER)),
                  pl.BlockSpec((None, tm, MLA_OUT), lambda b, i: (b, i, 0)),
                  pl.BlockSpec((1, SSM_INNER), lambda b, i: (0, 0)),
                  _mod_spec(d, layer, 5, None),
                  pl.BlockSpec((kdim, d), lambda b, i: (0, 0), pipeline_mode=pl.Buffered(1))],
        out_specs=pl.BlockSpec((None, tm, d), lambda b, i: (b, i, 0)),
        out_shape=jax.ShapeDtypeStruct((bsz, n, d), F32),
        compiler_params=_cparams(("parallel", "parallel"), 56),
        name="outproj",
    )(h, yf, yb, p, o, gain, mods, w)


def _pool_kernel(cur_ref, prev_ref, next_ref, sh_ref, sc_ref, gate_ref, w_ref, ps_ref, out_ref, *, n):
    i = pl.program_id(1)
    tm, d = cur_ref.shape
    pg = d // len(POOL_WINDOWS)
    cur = cur_ref[...]
    xe = jnp.concatenate([prev_ref[...], cur, next_ref[...]], axis=0)
    xe = _rms(xe) * (1.0 + sc_ref[...]) + sh_ref[...]
    pos = i * tm - HALO + lax.broadcasted_iota(jnp.int32, (tm + 2 * HALO, 1), 0)
    xe = jnp.where(jnp.logical_and(pos >= 0, pos < n), xe, 0.0)
    t = i * tm + lax.broadcasted_iota(jnp.int32, (tm, 1), 0)
    for gi, win in enumerate(POOL_WINDOWS):
        x = xe[:, gi * pg:(gi + 1) * pg]
        acc = x[0:tm + 2 * HALO - 1] + x[1:tm + 2 * HALO]
        span, off = 2, 1
        while span < win:
            rows = acc.shape[0] - span
            acc = acc[0:rows] + acc[span:span + rows]
            off += span // 2
            span *= 2
        wsum = acc[HALO - off:HALO - off + tm]
        cnt = (jnp.minimum(t + win // 2, n) - jnp.maximum(t - win // 2, 0)).astype(F32)
        pooled = (wsum / cnt - xe[HALO:HALO + tm, gi * pg:(gi + 1) * pg]).astype(BF16)
        mix = jnp.dot(pooled, w_ref[gi], preferred_element_type=F32) * ps_ref[:, gi * pg:(gi + 1) * pg]
        out_ref[:, gi * pg:(gi + 1) * pg] = cur[:, gi * pg:(gi + 1) * pg] + gate_ref[:, gi * pg:(gi + 1) * pg] * mix


def _pool(h, mods, layer, w, scale):
    bsz, n, d = h.shape
    tm = _tile(n, 256)
    pg = d // len(POOL_WINDOWS)
    hb = tm // HALO
    last_hb = n // HALO - 1
    return pl.pallas_call(
        functools.partial(_pool_kernel, n=n),
        grid=(bsz, n // tm),
        in_specs=[pl.BlockSpec((None, tm, d), lambda b, i: (b, i, 0)),
                  pl.BlockSpec((None, HALO, d), lambda b, i: (b, jnp.maximum(i * hb - 1, 0), 0)),
                  pl.BlockSpec((None, HALO, d), lambda b, i: (b, jnp.minimum((i + 1) * hb, last_hb), 0)),
                  _mod_spec(d, layer, 3, None), _mod_spec(d, layer, 4, None), _mod_spec(d, layer, 5, None),
                  pl.BlockSpec((len(POOL_WINDOWS), pg, pg), lambda b, i: (0, 0, 0)),
                  pl.BlockSpec((1, d), lambda b, i: (0, 0))],
        out_specs=pl.BlockSpec((None, tm, d), lambda b, i: (b, i, 0)),
        out_shape=jax.ShapeDtypeStruct((bsz, n, d), F32),
        compiler_params=_cparams(("parallel", "parallel"), 40),
        name="pool",
    )(h, h, h, mods, mods, mods, w, scale.reshape(1, d))


def _regroup_xbc(a):
    parts = []
    for g in range(SSM_GROUPS):
        parts += [a[..., g * GROUP_W:(g + 1) * GROUP_W],
                  a[..., SSM_INNER + g * SSM_STATE:SSM_INNER + (g + 1) * SSM_STATE],
                  a[..., SSM_INNER + SSM_GN + g * SSM_STATE:SSM_INNER + SSM_GN + (g + 1) * SSM_STATE]]
    return jnp.concatenate(parts, axis=-1)


def _prep_w_in(w_in):
    q_c, kv_c, k_r, z, xbc, dtr = jnp.split(
        w_in, np.cumsum((Q_LORA, KV_LORA, QK_ROPE, SSM_INNER, SSM_CONV_CH))[:].tolist(), axis=-1)
    w = jnp.concatenate([z, _regroup_xbc(xbc), q_c, kv_c], axis=-1)
    return w.astype(BF16), jnp.concatenate([k_r, dtr], axis=-1).astype(BF16)


def _prep_w_uq(w_uq):
    w = w_uq.reshape(Q_LORA, MLA_HEADS, QK_NOPE + QK_ROPE)
    w = jnp.pad(w, ((0, 0), (0, 0), (0, HEAD_PAD - QK_NOPE - QK_ROPE)))
    return w.reshape(Q_LORA, MLA_HEADS * HEAD_PAD).astype(BF16)


def kernel(x, c, ctx, c_ctx, mod_w, mod_b, ffn_w_gate, ffn_w_up, ffn_w_down, w_in, conv_w, conv_b, dt_bias, a_log,
           d_skip, ssm_norm, q_norm, kv_norm, w_uq, w_ukv, w_out, pool_w, pool_scale, final_norm):
    bsz, n, d = x.shape
    n_ctx = ctx.shape[1]
    n_total = n + n_ctx
    depth = mod_w.shape[0]
    assert bsz + 1 <= MOD_ROWS

    cond = jnp.concatenate([c, c_ctx[None, :], jnp.zeros((MOD_ROWS - bsz - 1, d), F32)], axis=0)
    mods = _adaln(cond, mod_w, mod_b).reshape(depth * MOD_ROWS * N_MOD, 1, d)
    wg, wu = _chunk_major_bf16(ffn_w_gate, ffn_w_up)
    wd = ffn_w_down.astype(BF16)

    h, hc = x, ctx
    for l in range(depth):
        j = l // 2
        last = l == depth - 1
        if l % 2 == 0:
            h = _ffn(h, mods, l, 0, None, wg, wu, wd)
            hc = _ffn(hc.reshape(1, bsz * n_ctx, d), mods, l, 0, bsz, wg, wu, wd).reshape(bsz, n_ctx, d)
            w_in_p, w_kd = _prep_w_in(w_in[j])
            pk = _inproj(h, mods, l, None, w_in_p, w_kd, n_total, 0)
            p, kd = _inproj(hc, mods, l, bsz, w_in_p, w_kd, n_total, n, prev=pk)
            conv_w8 = jnp.pad(_regroup_xbc(conv_w[j]), ((0, HALO - SSM_CONV), (0, 0)))
            dtb = jnp.pad(dt_bias[j].reshape(1, 2 * SSM_HEADS).astype(F32), ((0, 0), (DT_LANE, 0)))
            u, dtc = _conv(p, kd, conv_w8, _regroup_xbc(conv_b[j]).reshape(1, SSM_CONV_CH), dtb, a_log[j], n)
            yf, yb = _ssd(u, dtc, d_skip[j], n)
            tables = _rope_tables(n, n_total)
            q = _qproj(p, q_norm[j].reshape(1, Q_LORA), _prep_w_uq(w_uq[j]), tables, n)
            k, vt = _kvproj(p, kd, kv_norm[j].reshape(1, KV_LORA), w_ukv[j].astype(BF16), tables)
            o = _attention(q, k, vt)
            h = _outproj(h, yf, yb, p, o, ssm_norm[j].reshape(1, SSM_INNER), mods, l, w_out[j].astype(BF16))
        else:
            h = _ffn(h, mods, l, 0, None, wg, wu, wd)
            h = _pool(h, mods, l, pool_w[j].astype(BF16), pool_scale[j])
        h = _ffn(h, mods, l, 1, None, wg, wu, wd, final_gain=final_norm if last else None)
    return h
```

```python
import functools

import jax
import jax.numpy as jnp
import numpy as np
from jax import lax
from jax.experimental import pallas as pl
from jax.experimental.pallas import tpu as pltpu

F32 = jnp.float32
BF16 = jnp.bfloat16

EPS = 1e-6
N_MOD = 9
MOD_ROWS = 8
GRID_W = 64

SSM_HEADS = 32
SSM_HEAD_DIM = 64
SSM_INNER = SSM_HEADS * SSM_HEAD_DIM
SSM_GROUPS = 4
SSM_STATE = 128
SSM_CONV = 5
SSM_CHUNK = 128
SSM_GN = SSM_GROUPS * SSM_STATE
SSM_CONV_CH = SSM_INNER + 2 * SSM_GN
HEADS_PER_GROUP = SSM_HEADS // SSM_GROUPS
GROUP_W = HEADS_PER_GROUP * SSM_HEAD_DIM
XBC_GROUP_W = GROUP_W + 2 * SSM_STATE

MLA_HEADS = 16
Q_LORA = 512
KV_LORA = 512
QK_NOPE = 128
QK_ROPE = 64
V_HEAD = 128
MLA_OUT = MLA_HEADS * V_HEAD
MLA_SCALE = (QK_NOPE + QK_ROPE) ** -0.5
Q_SCALE = MLA_SCALE * 1.4426950408889634
ROPE_THETA = 10000.0
ROPE_HALF = QK_ROPE // 2
ROPE_AXIS_FREQS = QK_ROPE // 4
HEAD_PAD = 256

LANE = 128
HALO = 8
HALO16 = 16

P_Z = 0
P_XBC = P_Z + SSM_INNER
P_QC = P_XBC + SSM_CONV_CH
P_KVC = P_QC + Q_LORA
DT_LANE = QK_ROPE

POOL_WINDOWS = (2, 4, 8, 16)


def _tile(dim, pref):
    t = min(dim, pref)
    while dim % t:
        t //= 2
    return t


def _cparams(sem, vmem_mb):
    return pltpu.CompilerParams(dimension_semantics=sem, vmem_limit_bytes=vmem_mb << 20)


def _rms(x):
    return x * lax.rsqrt(jnp.mean(x * x, axis=-1, keepdims=True) + EPS)


def _mod_spec(d, layer, k, row):
    if row is None:
        return pl.BlockSpec((None, 1, d), lambda b, *_: ((layer * MOD_ROWS + b) * N_MOD + k, 0, 0))
    return pl.BlockSpec((None, 1, d), lambda b, *_: ((layer * MOD_ROWS + row) * N_MOD + k, 0, 0))


def _adaln_kernel(c_ref, w_ref, b_ref, o_ref):
    a = jax.nn.silu(c_ref[...]).astype(BF16)
    o_ref[...] = jnp.dot(a, w_ref[...].astype(BF16), preferred_element_type=F32) + b_ref[...]


def _adaln(cond, mod_w, mod_b):
    depth, d, nd = mod_w.shape
    tn = _tile(nd, 1024)
    return pl.pallas_call(
        _adaln_kernel,
        grid=(depth, nd // tn),
        in_specs=[pl.BlockSpec((MOD_ROWS, d), lambda l, j: (0, 0)),
                  pl.BlockSpec((None, d, tn), lambda l, j: (l, 0, j)),
                  pl.BlockSpec((None, 1, tn), lambda l, j: (l, 0, j))],
        out_specs=pl.BlockSpec((None, MOD_ROWS, tn), lambda l, j: (l, 0, j)),
        out_shape=jax.ShapeDtypeStruct((depth, MOD_ROWS, nd), F32),
        compiler_params=_cparams(("parallel", "parallel"), 40),
        name="adaln",
    )(cond, mod_w, mod_b.reshape(depth, 1, nd))


def _cast_kernel(a_ref, b_ref, oa_ref, ob_ref):
    oa_ref[...] = a_ref[...].astype(BF16)
    ob_ref[...] = b_ref[...].astype(BF16)


def _chunk_major_bf16(wg, wu):
    depth, two, d, f = wg.shape
    tf = _tile(f, 512)
    src = pl.BlockSpec((None, None, d, tf), lambda l, s, j: (l, s, 0, j))
    dst = pl.BlockSpec((None, None, None, d, tf), lambda l, s, j: (l, s, j, 0, 0))
    shape = jax.ShapeDtypeStruct((depth, two, f // tf, d, tf), BF16)
    return pl.pallas_call(
        _cast_kernel,
        grid=(depth, two, f // tf),
        in_specs=[src, src],
        out_specs=[dst, dst],
        out_shape=[shape, shape],
        compiler_params=_cparams(("parallel", "parallel", "parallel"), 40),
        name="wcast",
    )(wg, wu)


def _ffn_kernel(*refs, final_norm):
    if final_norm:
        h_ref, sh_ref, sc_ref, gt_ref, wg_ref, wu_ref, wd_ref, fn_ref, o_ref, u_scr = refs
    else:
        h_ref, sh_ref, sc_ref, gt_ref, wg_ref, wu_ref, wd_ref, o_ref, u_scr = refs
    j = pl.program_id(2)

    @pl.when(j == 0)
    def _():
        u = _rms(h_ref[...]) * (1.0 + sc_ref[...]) + sh_ref[...]
        u_scr[...] = u.astype(BF16)
        o_ref[...] = jnp.zeros_like(o_ref)

    u = u_scr[...]
    g = jnp.dot(u, wg_ref[...], preferred_element_type=F32)
    up = jnp.dot(u, wu_ref[...], preferred_element_type=F32)
    a = (jax.nn.silu(g) * up).astype(BF16)
    o_ref[...] += jnp.dot(a, wd_ref[...], preferred_element_type=F32)

    @pl.when(j == pl.num_programs(2) - 1)
    def _():
        out = h_ref[...] + (0.5 * gt_ref[...]) * o_ref[...]
        if final_norm:
            out = _rms(out) * fn_ref[...]
        o_ref[...] = out


def _ffn(h, mods, layer, half, row, wg, wu, wd, final_gain=None):
    bsz, n, d = h.shape
    nj, tf = wg.shape[2], wg.shape[-1]
    tm = _tile(n, 512)
    k0 = 6 * half
    in_specs = [pl.BlockSpec((None, tm, d), lambda b, i, j: (b, i, 0)),
                _mod_spec(d, layer, k0, row), _mod_spec(d, layer, k0 + 1, row), _mod_spec(d, layer, k0 + 2, row),
                pl.BlockSpec((None, None, None, d, tf), lambda b, i, j: (layer, half, j, 0, 0)),
                pl.BlockSpec((None, None, None, d, tf), lambda b, i, j: (layer, half, j, 0, 0)),
                pl.BlockSpec((None, None, tf, d), lambda b, i, j: (layer, half, j, 0))]
    args = [h, mods, mods, mods, wg, wu, wd]
    if final_gain is not None:
        in_specs.append(pl.BlockSpec((1, d), lambda b, i, j: (0, 0)))
        args.append(final_gain.reshape(1, d))
    return pl.pallas_call(
        functools.partial(_ffn_kernel, final_norm=final_gain is not None),
        grid=(bsz, n // tm, nj),
        in_specs=in_specs,
        out_specs=pl.BlockSpec((None, tm, d), lambda b, i, j: (b, i, 0)),
        out_shape=jax.ShapeDtypeStruct((bsz, n, d), F32),
        scratch_shapes=[pltpu.VMEM((tm, d), BF16)],
        compiler_params=_cparams(("parallel", "parallel", "arbitrary"), 48),
        name="ffn",
    )(*args)


def _inproj_kernel(h_ref, sh_ref, sc_ref, w_ref, wkd_ref, *rest):
    o_ref, kd_ref, u_scr = rest[-3:]

    @pl.when(pl.program_id(2) == 0)
    def _():
        u = (_rms(h_ref[...]) * (1.0 + sc_ref[...]) + sh_ref[...]).astype(BF16)
        u_scr[...] = u
        kd_ref[...] = jnp.dot(u, wkd_ref[...], preferred_element_type=F32)

    o_ref[...] = jnp.dot(u_scr[...], w_ref[...], preferred_element_type=F32).astype(BF16)


def _inproj(h, mods, layer, row, w, wkd, n_total, row_off, prev=None):
    bsz, n, d = h.shape
    cols = w.shape[1]
    tm, tn = _tile(n, 1024), _tile(cols, 1536)
    off = row_off // tm
    in_specs = [pl.BlockSpec((None, tm, d), lambda b, i, j: (b, i, 0)),
                _mod_spec(d, layer, 3, row), _mod_spec(d, layer, 4, row),
                pl.BlockSpec((d, tn), lambda b, i, j: (0, j)),
                pl.BlockSpec((d, LANE), lambda b, i, j: (0, 0))]
    args = [h, mods, mods, w, wkd]
    aliases = {}
    if prev is not None:
        in_specs += [pl.BlockSpec(memory_space=pl.ANY)] * 2
        args += list(prev)
        aliases = {5: 0, 6: 1}
    return pl.pallas_call(
        _inproj_kernel,
        grid=(bsz, n // tm, cols // tn),
        in_specs=in_specs,
        out_specs=[pl.BlockSpec((None, tm, tn), lambda b, i, j: (b, off + i, j)),
                   pl.BlockSpec((None, tm, LANE), lambda b, i, j: (b, off + i, 0))],
        out_shape=[jax.ShapeDtypeStruct((bsz, n_total, cols), BF16),
                   jax.ShapeDtypeStruct((bsz, n_total, LANE), F32)],
        scratch_shapes=[pltpu.VMEM((tm, d), BF16)],
        input_output_aliases=aliases,
        compiler_params=_cparams(("parallel", "parallel", "arbitrary"), 48),
        name="inproj",
    )(*args)


def _exact_dot(lhs, rhs, lhs_split):
    val = lhs if lhs_split else rhs
    acc = None
    for i in range(3):
        piece = val.astype(BF16)
        if lhs_split:
            d = jnp.dot(piece, rhs, preferred_element_type=F32)
        else:
            d = jnp.dot(lhs, piece, preferred_element_type=F32)
        acc = d if acc is None else acc + d
        if i < 2:
            val = val - piece.astype(F32)
    return acc


def _conv_kernel(cur_ref, prev_ref, next_ref, dtr_ref, w_ref, b_ref, dtb_ref, aneg_ref, tri_ref,
                 u_ref, dtc_ref, *, lat_tiles):
    t = pl.program_id(1)
    tc = cur_ref.shape[0]
    first = jnp.logical_or(t == 0, t == lat_tiles)
    last = jnp.logical_or(t == lat_tiles - 1, t == pl.num_programs(1) - 1)
    prev = jnp.where(first, 0.0, prev_ref[...].astype(F32))
    nxt = jnp.where(last, 0.0, next_ref[...].astype(F32))
    xe = jnp.concatenate([prev, cur_ref[...].astype(F32), nxt], axis=0)
    w = w_ref[...]
    ext = tc + 2 * HALO16
    acc = b_ref[...] + w[2:3, :] * xe[HALO16:HALO16 + tc, :]
    for k in (0, 1, 3, 4):
        acc = acc + w[k:k + 1, :] * pltpu.roll(xe, (2 - k) % ext, 0)[HALO16:HALO16 + tc, :]
    u_ref[...] = jax.nn.silu(acc).astype(BF16)

    @pl.when(pl.program_id(2) == 0)
    def _():
        dt = jax.nn.softplus(dtr_ref[...] + dtb_ref[...])
        dtc_ref[:, :LANE] = dt
        a = dt * aneg_ref[...]
        fwd_lane = lax.broadcasted_iota(jnp.int32, (SSM_CHUNK, LANE), 1) < DT_LANE + SSM_HEADS
        for r in range(tc // SSM_CHUNK):
            rows = slice(r * SSM_CHUNK, (r + 1) * SSM_CHUNK)
            pre = _exact_dot(tri_ref[0], a[rows], lhs_split=False)
            suf = _exact_dot(tri_ref[1], a[rows], lhs_split=False)
            dtc_ref[rows, LANE:] = jnp.where(fwd_lane, pre, suf)


def _conv(p, kd, conv_w8, conv_b, dt_bias_row, a_log, n_lat):
    bsz, n_total, _ = p.shape
    tc = 256
    cw = 1024
    L = SSM_CHUNK
    aneg = jnp.pad(-jnp.exp(a_log.astype(F32)).reshape(1, 2 * SSM_HEADS), ((0, 0), (DT_LANE, 0)))
    tri = jnp.asarray(np.stack([np.tril(np.ones((L, L), np.float32)), np.triu(np.ones((L, L), np.float32))]), BF16)
    assert n_lat % tc == 0 and n_total % tc == 0 and SSM_CONV_CH % cw == 0 and P_XBC % cw == 0
    c0 = P_XBC // cw
    hb = tc // HALO16
    last_hb = n_total // HALO16 - 1
    return pl.pallas_call(
        functools.partial(_conv_kernel, lat_tiles=n_lat // tc),
        grid=(bsz, n_total // tc, SSM_CONV_CH // cw),
        in_specs=[pl.BlockSpec((None, tc, cw), lambda b, t, c: (b, t, c0 + c)),
                  pl.BlockSpec((None, HALO16, cw), lambda b, t, c: (b, jnp.maximum(t * hb - 1, 0), c0 + c)),
                  pl.BlockSpec((None, HALO16, cw), lambda b, t, c: (b, jnp.minimum((t + 1) * hb, last_hb), c0 + c)),
                  pl.BlockSpec((None, tc, LANE), lambda b, t, c: (b, t, 0)),
                  pl.BlockSpec((HALO, cw), lambda b, t, c: (0, c)),
                  pl.BlockSpec((1, cw), lambda b, t, c: (0, c)),
                  pl.BlockSpec((1, LANE), lambda b, t, c: (0, 0)),
                  pl.BlockSpec((1, LANE), lambda b, t, c: (0, 0)),
                  pl.BlockSpec((2, L, L), lambda b, t, c: (0, 0, 0))],
        out_specs=[pl.BlockSpec((None, tc, cw), lambda b, t, c: (b, t, c)),
                   pl.BlockSpec((None, tc, 2 * LANE), lambda b, t, c: (b, t, 0))],
        out_shape=[jax.ShapeDtypeStruct((bsz, n_total, SSM_CONV_CH), BF16),
                   jax.ShapeDtypeStruct((bsz, n_total, 2 * LANE), F32)],
        compiler_params=_cparams(("parallel", "parallel", "arbitrary"), 32),
        name="conv",
    )(p, p, p, kd, conv_w8, conv_b, dt_bias_row, aneg, tri)


def _ssd_scalars(dtc_ref, exp_ref, fwd):
    L = SSM_CHUNK
    dcum = dtc_ref[:, LANE:]
    edc = jnp.exp(dcum)
    dct = dcum.T
    dtt = dtc_ref[:, :LANE].T
    tot = L - 1 if fwd else 0
    wt = dtt * jnp.exp(dct[:, tot:tot + 1] - dct)
    etot = jnp.broadcast_to(edc[tot:tot + 1, :], (HALO, LANE))
    etot = _exact_dot(etot, exp_ref[...], lhs_split=True)[0:1, :]
    return dcum, edc, dct, dtt, wt, etot


def _ssd_direction(xbc_ref, scalars, st_ref, y_ref, dsk_ref, fwd):
    L = SSM_CHUNK
    dcum, edc, dct, dtt, wt, etot = scalars
    ri = lax.broadcasted_iota(jnp.int32, (L, L), 0)
    ci = lax.broadcasted_iota(jnp.int32, (L, L), 1)
    keep = (ri >= ci) if fwd else (ri <= ci)
    left = lax.broadcasted_iota(jnp.int32, (L, LANE), 1) < SSM_HEAD_DIM
    for g in range(SSM_GROUPS):
        c0 = g * XBC_GROUP_W
        xs = xbc_ref[:, c0:c0 + GROUP_W]
        bm = xbc_ref[:, c0 + GROUP_W:c0 + GROUP_W + SSM_STATE]
        cm = xbc_ref[:, c0 + GROUP_W + SSM_STATE:c0 + XBC_GROUP_W]
        cb = lax.dot_general(cm, bm, (((1,), (1,)), ((), ())), preferred_element_type=F32)
        cm32 = cm.astype(F32)
        bt32 = bm.astype(F32).T
        st = st_ref[:, g * GROUP_W:(g + 1) * GROUP_W]
        st16 = st.astype(BF16)
        snew = []
        for jp in range(HEADS_PER_GROUP // 2):
            xpair = xs[:, jp * LANE:(jp + 1) * LANE]
            rhs = jnp.concatenate([xpair, st16[:, jp * LANE:(jp + 1) * LANE]], axis=0)
            yp, sp = [], []
            for j in (2 * jp, 2 * jp + 1):
                hl = DT_LANE + (0 if fwd else SSM_HEADS) + g * HEADS_PER_GROUP + j
                seg = jnp.where(keep, dcum[:, hl:hl + 1] - dct[hl:hl + 1, :], -jnp.inf)
                gm = cb * jnp.exp(seg) * dtt[hl:hl + 1, :]
                ce = cm32 * edc[:, hl:hl + 1]
                lhs = jnp.concatenate([gm.astype(BF16), ce.astype(BF16)], axis=1)
                yp.append(jnp.dot(lhs, rhs, preferred_element_type=F32))
                btj = (bt32 * wt[hl:hl + 1, :]).astype(BF16)
                sp.append(jnp.dot(btj, xpair, preferred_element_type=F32))
            ypair = jnp.where(left, yp[0], yp[1])
            cols = slice(g * GROUP_W + jp * LANE, g * GROUP_W + (jp + 1) * LANE)
            if dsk_ref is not None:
                ypair = ypair + dsk_ref[:, cols] * xpair.astype(F32)
            y_ref[:, cols] = ypair
            snew.append(jnp.where(left, sp[0], sp[1]))
        st_ref[:, g * GROUP_W:(g + 1) * GROUP_W] = (etot[:, g * GROUP_W:(g + 1) * GROUP_W] * st
                                                    + jnp.concatenate(snew, axis=1))


def _ssd_kernel(xf_ref, dtf_ref, xb_ref, dtb_ref, exp_ref, dsk_ref, yf_ref, yb_ref, stf_scr, stb_scr):
    @pl.when(pl.program_id(1) == 0)
    def _():
        stf_scr[...] = jnp.zeros_like(stf_scr)
        stb_scr[...] = jnp.zeros_like(stb_scr)

    sc_f = _ssd_scalars(dtf_ref, exp_ref.at[0], True)
    sc_b = _ssd_scalars(dtb_ref, exp_ref.at[1], False)
    _ssd_direction(xf_ref, sc_f, stf_scr, yf_ref, dsk_ref, True)
    _ssd_direction(xb_ref, sc_b, stb_scr, yb_ref, None, False)


def _ssd(u, dtc, d_skip, n_lat):
    bsz, n_total, _ = u.shape
    L = SSM_CHUNK
    nct, ncl = n_total // L, n_lat // L
    ncc = nct - ncl
    expand = np.zeros((2, LANE, SSM_INNER), np.float32)
    for d in range(2):
        for h in range(SSM_HEADS):
            expand[d, DT_LANE + d * SSM_HEADS + h, h * SSM_HEAD_DIM:(h + 1) * SSM_HEAD_DIM] = 1.0
    dsk = jnp.repeat(d_skip.astype(F32), SSM_HEAD_DIM).reshape(1, SSM_INNER)

    def cf(s):
        return (s + ncl) % nct

    def cbk(s):
        return nct - 1 - s

    def chunk_specs(cmap):
        return [pl.BlockSpec((None, L, SSM_CONV_CH), lambda b, s: (b, cmap(s), 0)),
                pl.BlockSpec((None, L, 2 * LANE), lambda b, s: (b, cmap(s), 0))]

    in_specs = chunk_specs(cf) + chunk_specs(cbk) + [
        pl.BlockSpec((2, LANE, SSM_INNER), lambda b, s: (0, 0, 0)),
        pl.BlockSpec((1, SSM_INNER), lambda b, s: (0, 0))]
    out_specs = [pl.BlockSpec((None, L, SSM_INNER), lambda b, s: (b, jnp.maximum(s - ncc, 0), 0)),
                 pl.BlockSpec((None, L, SSM_INNER), lambda b, s: (b, jnp.minimum(cbk(s), ncl - 1), 0))]
    return pl.pallas_call(
        _ssd_kernel,
        grid=(bsz, nct),
        in_specs=in_specs,
        out_specs=out_specs,
        out_shape=[jax.ShapeDtypeStruct((bsz, n_lat, SSM_INNER), F32)] * 2,
        scratch_shapes=[pltpu.VMEM((SSM_STATE, SSM_INNER), F32), pltpu.VMEM((SSM_STATE, SSM_INNER), F32)],
        compiler_params=_cparams(("parallel", "arbitrary"), 32),
        name="ssd",
    )(u, dtc, u, dtc, jnp.asarray(expand, BF16), dsk)


def _rope(blk, cos_ref, s1_ref, s2_ref):
    return (blk * cos_ref[...] + pltpu.roll(blk, LANE - ROPE_HALF, 1) * s1_ref[...]
            + pltpu.roll(blk, ROPE_HALF, 1) * s2_ref[...])


def _qproj_kernel(qc_ref, g_ref, w_ref, cos_ref, s1_ref, s2_ref, q_ref):
    y = (_rms(qc_ref[...].astype(F32)) * g_ref[...]).astype(BF16)
    for h in range(MLA_HEADS):
        qh = jnp.dot(y, w_ref[:, h * HEAD_PAD:(h + 1) * HEAD_PAD], preferred_element_type=F32) * Q_SCALE
        q_ref[:, h * HEAD_PAD:h * HEAD_PAD + LANE] = qh[:, :LANE].astype(BF16)
        q_ref[:, h * HEAD_PAD + LANE:(h + 1) * HEAD_PAD] = _rope(qh[:, LANE:], cos_ref, s1_ref, s2_ref).astype(BF16)


def _kvproj_kernel(kvc_ref, kr_ref, g_ref, w_ref, cos_ref, s1_ref, s2_ref, k_ref, vt_ref):
    y = (_rms(kvc_ref[...].astype(F32)) * g_ref[...]).astype(BF16)
    kr = _rope(kr_ref[...], cos_ref, s1_ref, s2_ref).astype(BF16)
    for h in range(MLA_HEADS):
        kv = jnp.dot(y, w_ref[:, h * HEAD_PAD:(h + 1) * HEAD_PAD], preferred_element_type=F32)
        k_ref[:, h * HEAD_PAD:h * HEAD_PAD + LANE] = kv[:, :QK_NOPE].astype(BF16)
        k_ref[:, h * HEAD_PAD + LANE:(h + 1) * HEAD_PAD] = kr
        vt_ref[h * V_HEAD:(h + 1) * V_HEAD, :] = kv[:, QK_NOPE:].T.astype(BF16)


def _rope_tables(n_lat, n_total):
    rows = n_lat // GRID_W
    row = np.repeat(np.arange(rows, dtype=np.float32), GRID_W)
    col = np.tile(np.arange(GRID_W, dtype=np.float32), rows)
    inv = jnp.asarray(ROPE_THETA, F32) ** (-jnp.arange(ROPE_AXIS_FREQS, dtype=F32) / ROPE_AXIS_FREQS)
    ang = jnp.concatenate([jnp.asarray(row)[:, None] * inv, jnp.asarray(col)[:, None] * inv], axis=-1)
    ang = jnp.pad(ang, ((0, n_total - n_lat), (0, 0)))
    cos, sin = jnp.cos(ang), jnp.sin(ang)
    z32 = jnp.zeros_like(sin)
    z64 = jnp.zeros((n_total, LANE - QK_ROPE), F32)
    return (jnp.concatenate([cos, cos, z64], axis=1),
            jnp.concatenate([-sin, z32, z64], axis=1),
            jnp.concatenate([z32, sin, z64], axis=1))


def _qproj(p, gain, w, tables, n_lat):
    bsz = p.shape[0]
    tm = _tile(n_lat, 512)
    tab = pl.BlockSpec((tm, LANE), lambda b, i: (i, 0))
    return pl.pallas_call(
        _qproj_kernel,
        grid=(bsz, n_lat // tm),
        in_specs=[pl.BlockSpec((None, tm, Q_LORA), lambda b, i: (b, i, P_QC // Q_LORA)),
                  pl.BlockSpec((1, Q_LORA), lambda b, i: (0, 0)),
                  pl.BlockSpec((Q_LORA, MLA_HEADS * HEAD_PAD), lambda b, i: (0, 0)),
                  tab, tab, tab],
        out_specs=pl.BlockSpec((None, tm, MLA_HEADS * HEAD_PAD), lambda b, i: (b, i, 0)),
        out_shape=jax.ShapeDtypeStruct((bsz, n_lat, MLA_HEADS * HEAD_PAD), BF16),
        compiler_params=_cparams(("parallel", "parallel"), 40),
        name="qproj",
    )(p, gain, w, *tables)


def _kvproj(p, kd, gain, w, tables):
    bsz, n_total, _ = p.shape
    tm = _tile(n_total, 256)
    tab = pl.BlockSpec((tm, LANE), lambda b, i: (i, 0))
    return pl.pallas_call(
        _kvproj_kernel,
        grid=(bsz, n_total // tm),
        in_specs=[pl.BlockSpec((None, tm, KV_LORA), lambda b, i: (b, i, P_KVC // KV_LORA)),
                  pl.BlockSpec((None, tm, LANE), lambda b, i: (b, i, 0)),
                  pl.BlockSpec((1, KV_LORA), lambda b, i: (0, 0)),
                  pl.BlockSpec((KV_LORA, MLA_HEADS * HEAD_PAD), lambda b, i: (0, 0)),
                  tab, tab, tab],
        out_specs=[pl.BlockSpec((None, tm, MLA_HEADS * HEAD_PAD), lambda b, i: (b, i, 0)),
                   pl.BlockSpec((None, MLA_OUT, tm), lambda b, i: (b, 0, i))],
        out_shape=[jax.ShapeDtypeStruct((bsz, n_total, MLA_HEADS * HEAD_PAD), BF16),
                   jax.ShapeDtypeStruct((bsz, MLA_OUT, n_total), BF16)],
        compiler_params=_cparams(("parallel", "parallel"), 40),
        name="kvproj",
    )(p, kd, gain, w, *tables)


def _attn_kernel(q_ref, k_ref, vt_ref, o_ref, *, sub, qsub):
    nk = k_ref.shape[0]
    bounds = []
    for k0 in range(0, nk, sub):
        k1 = nk if nk - k0 < 2 * sub else k0 + sub
        bounds.append((k0, k1))
        if k1 == nk:
            break

    def weighted(b, p):
        vt = jnp.concatenate([vt_ref[:, b[0]:b[1]], jnp.ones((HALO16, b[1] - b[0]), BF16)], axis=0)
        return jnp.dot(vt, p.astype(BF16), preferred_element_type=F32)

    for c0 in range(0, q_ref.shape[0], qsub):
        q = q_ref[c0:c0 + qsub, :]

        def scores(b):
            return lax.dot_general(k_ref[b[0]:b[1], :], q, (((1,), (1,)), ((), ())), preferred_element_type=F32)

        m = acc = None
        s_next = scores(bounds[0])
        for bi, b in enumerate(bounds):
            s = s_next
            if bi + 1 < len(bounds):
                s_next = scores(bounds[bi + 1])
            ms = jnp.max(s, axis=0, keepdims=True)
            if m is None:
                m = ms
                acc = weighted(b, jnp.exp2(s - m))
            else:
                mn = jnp.maximum(m, ms)
                acc = jnp.exp2(m - mn) * acc + weighted(b, jnp.exp2(s - mn))
                m = mn
        o_ref[c0:c0 + qsub, :] = (acc[:V_HEAD] / acc[V_HEAD:V_HEAD + 1]).T.astype(BF16)


def _attention(q, k, vt):
    bsz, n_lat, _ = q.shape
    n_total = k.shape[1]
    tq = _tile(n_lat, 1024)
    return pl.pallas_call(
        functools.partial(_attn_kernel, sub=1024, qsub=tq),
        grid=(bsz, MLA_HEADS, n_lat // tq),
        in_specs=[pl.BlockSpec((None, tq, HEAD_PAD), lambda b, h, i: (b, i, h)),
                  pl.BlockSpec((None, n_total, HEAD_PAD), lambda b, h, i: (b, 0, h)),
                  pl.BlockSpec((None, V_HEAD, n_total), lambda b, h, i: (b, h, 0))],
        out_specs=pl.BlockSpec((None, tq, V_HEAD), lambda b, h, i: (b, i, h)),
        out_shape=jax.ShapeDtypeStruct((bsz, n_lat, MLA_OUT), BF16),
        compiler_params=_cparams(("parallel", "parallel", "parallel"), 48),
        name="attn",
    )(q, k, vt)


def _outproj_kernel(h_ref, yf_ref, yb_ref, z_ref, o_ref, gain_ref, gate_ref, w_ref, out_ref):
    gw = SSM_INNER // SSM_GROUPS
    mix = jnp.dot(o_ref[...], w_ref[SSM_INNER:, :], preferred_element_type=F32)
    for gi in range(SSM_GROUPS):
        sl = slice(gi * gw, (gi + 1) * gw)
        g = (yf_ref[:, sl] + yb_ref[:, sl]) * jax.nn.silu(z_ref[:, sl].astype(F32))
        lhs = (_rms(g) * gain_ref[:, sl]).astype(BF16)
        mix = mix + jnp.dot(lhs, w_ref[sl, :], preferred_element_type=F32)
    out_ref[...] = h_ref[...] + gate_ref[...] * mix


def _outproj(h, yf, yb, p, o, gain, mods, layer, w):
    bsz, n, d = h.shape
    kdim = w.shape[0]
    tm = _tile(n, 256)
    return pl.pallas_call(
        _outproj_kernel,
        grid=(bsz, n // tm),
        in_specs=[pl.BlockSpec((None, tm, d), lambda b, i: (b, i, 0)),
                  pl.BlockSpec((None, tm, SSM_INNER), lambda b, i: (b, i, 0)),
                  pl.BlockSpec((None, tm, SSM_INNER), lambda b, i: (b, i, 0)),
                  pl.BlockSpec((None, tm, SSM_INNER), lambda b, i: (b, i, P_Z // SSM_INNER)),
                  pl.BlockSpec((None, tm, MLA_OUT), lambda b, i: (b, i, 0)),
                  pl.BlockSpec((1, SSM_INNER), lambda b, i: (0, 0)),
                  _mod_spec(d, layer, 5, None),
                  pl.BlockSpec((kdim, d), lambda b, i: (0, 0), pipeline_mode=pl.Buffered(1))],
        out_specs=pl.BlockSpec((None, tm, d), lambda b, i: (b, i, 0)),
        out_shape=jax.ShapeDtypeStruct((bsz, n, d), F32),
        compiler_params=_cparams(("parallel", "parallel"), 56),
        name="outproj",
    )(h, yf, yb, p, o, gain, mods, w)


def _pool_kernel(cur_ref, prev_ref, next_ref, sh_ref, sc_ref, gate_ref, w_ref, ps_ref, out_ref, *, n):
    i = pl.program_id(1)
    tm, d = cur_ref.shape
    pg = d // len(POOL_WINDOWS)
    cur = cur_ref[...]
    xe = jnp.concatenate([prev_ref[...], cur, next_ref[...]], axis=0)
    xe = _rms(xe) * (1.0 + sc_ref[...]) + sh_ref[...]
    pos = i * tm - HALO + lax.broadcasted_iota(jnp.int32, (tm + 2 * HALO, 1), 0)
    xe = jnp.where(jnp.logical_and(pos >= 0, pos < n), xe, 0.0)
    t = i * tm + lax.broadcasted_iota(jnp.int32, (tm, 1), 0)
    for gi, win in enumerate(POOL_WINDOWS):
        x = xe[:, gi * pg:(gi + 1) * pg]
        acc = x[0:tm + 2 * HALO - 1] + x[1:tm + 2 * HALO]
        span, off = 2, 1
        while span < win:
            rows = acc.shape[0] - span
            acc = acc[0:rows] + acc[span:span + rows]
            off += span // 2
            span *= 2
        wsum = acc[HALO - off:HALO - off + tm]
        cnt = (jnp.minimum(t + win // 2, n) - jnp.maximum(t - win // 2, 0)).astype(F32)
        pooled = (wsum / cnt - xe[HALO:HALO + tm, gi * pg:(gi + 1) * pg]).astype(BF16)
        mix = jnp.dot(pooled, w_ref[gi], preferred_element_type=F32) * ps_ref[:, gi * pg:(gi + 1) * pg]
        out_ref[:, gi * pg:(gi + 1) * pg] = cur[:, gi * pg:(gi + 1) * pg] + gate_ref[:, gi * pg:(gi + 1) * pg] * mix


def _pool(h, mods, layer, w, scale):
    bsz, n, d = h.shape
    tm = _tile(n, 256)
    pg = d // len(POOL_WINDOWS)
    hb = tm // HALO
    last_hb = n // HALO - 1
    return pl.pallas_call(
        functools.partial(_pool_kernel, n=n),
        grid=(bsz, n // tm),
        in_specs=[pl.BlockSpec((None, tm, d), lambda b, i: (b, i, 0)),
                  pl.BlockSpec((None, HALO, d), lambda b, i: (b, jnp.maximum(i * hb - 1, 0), 0)),
                  pl.BlockSpec((None, HALO, d), lambda b, i: (b, jnp.minimum((i + 1) * hb, last_hb), 0)),
                  _mod_spec(d, layer, 3, None), _mod_spec(d, layer, 4, None), _mod_spec(d, layer, 5, None),
                  pl.BlockSpec((len(POOL_WINDOWS), pg, pg), lambda b, i: (0, 0, 0)),
                  pl.BlockSpec((1, d), lambda b, i: (0, 0))],
        out_specs=pl.BlockSpec((None, tm, d), lambda b, i: (b, i, 0)),
        out_shape=jax.ShapeDtypeStruct((bsz, n, d), F32),
        compiler_params=_cparams(("parallel", "parallel"), 40),
        name="pool",
    )(h, h, h, mods, mods, mods, w, scale.reshape(1, d))


def _regroup_xbc(a):
    parts = []
    for g in range(SSM_GROUPS):
        parts += [a[..., g * GROUP_W:(g + 1) * GROUP_W],
                  a[..., SSM_INNER + g * SSM_STATE:SSM_INNER + (g + 1) * SSM_STATE],
                  a[..., SSM_INNER + SSM_GN + g * SSM_STATE:SSM_INNER + SSM_GN + (g + 1) * SSM_STATE]]
    return jnp.concatenate(parts, axis=-1)


def _prep_w_in(w_in):
    q_c, kv_c, k_r, z, xbc, dtr = jnp.split(
        w_in, np.cumsum((Q_LORA, KV_LORA, QK_ROPE, SSM_INNER, SSM_CONV_CH))[:].tolist(), axis=-1)
    w = jnp.concatenate([z, _regroup_xbc(xbc), q_c, kv_c], axis=-1)
    return w.astype(BF16), jnp.concatenate([k_r, dtr], axis=-1).astype(BF16)


def _prep_w_uq(w_uq):
    w = w_uq.reshape(Q_LORA, MLA_HEADS, QK_NOPE + QK_ROPE)
    w = jnp.pad(w, ((0, 0), (0, 0), (0, HEAD_PAD - QK_NOPE - QK_ROPE)))
    return w.reshape(Q_LORA, MLA_HEADS * HEAD_PAD).astype(BF16)


def kernel(x, c, ctx, c_ctx, mod_w, mod_b, ffn_w_gate, ffn_w_up, ffn_w_down, w_in, conv_w, conv_b, dt_bias, a_log,
           d_skip, ssm_norm, q_norm, kv_norm, w_uq, w_ukv, w_out, pool_w, pool_scale, final_norm):
    bsz, n, d = x.shape
    n_ctx = ctx.shape[1]
    n_total = n + n_ctx
    depth = mod_w.shape[0]
    assert bsz + 1 <= MOD_ROWS

    cond = jnp.concatenate([c, c_ctx[None, :], jnp.zeros((MOD_ROWS - bsz - 1, d), F32)], axis=0)
    mods = _adaln(cond, mod_w, mod_b).reshape(depth * MOD_ROWS * N_MOD, 1, d)
    wg, wu = _chunk_major_bf16(ffn_w_gate, ffn_w_up)
    wd = ffn_w_down.astype(BF16)

    h, hc = x, ctx
    for l in range(depth):
        j = l // 2
        last = l == depth - 1
        if l % 2 == 0:
            h = _ffn(h, mods, l, 0, None, wg, wu, wd)
            hc = _ffn(hc.reshape(1, bsz * n_ctx, d), mods, l, 0, bsz, wg, wu, wd).reshape(bsz, n_ctx, d)
            w_in_p, w_kd = _prep_w_in(w_in[j])
            pk = _inproj(h, mods, l, None, w_in_p, w_kd, n_total, 0)
            p, kd = _inproj(hc, mods, l, bsz, w_in_p, w_kd, n_total, n, prev=pk)
            conv_w8 = jnp.pad(_regroup_xbc(conv_w[j]), ((0, HALO - SSM_CONV), (0, 0)))
            dtb = jnp.pad(dt_bias[j].reshape(1, 2 * SSM_HEADS).astype(F32), ((0, 0), (DT_LANE, 0)))
            u, dtc = _conv(p, kd, conv_w8, _regroup_xbc(conv_b[j]).reshape(1, SSM_CONV_CH), dtb, a_log[j], n)
            yf, yb = _ssd(u, dtc, d_skip[j], n)
            tables = _rope_tables(n, n_total)
            q = _qproj(p, q_norm[j].reshape(1, Q_LORA), _prep_w_uq(w_uq[j]), tables, n)
            k, vt = _kvproj(p, kd, kv_norm[j].reshape(1, KV_LORA), w_ukv[j].astype(BF16), tables)
            o = _attention(q, k, vt)
            h = _outproj(h, yf, yb, p, o, ssm_norm[j].reshape(1, SSM_INNER), mods, l, w_out[j].astype(BF16))
        else:
            h = _ffn(h, mods, l, 0, None, wg, wu, wd)
            h = _pool(h, mods, l, pool_w[j].astype(BF16), pool_scale[j])
        h = _ffn(h, mods, l, 1, None, wg, wu, wd, final_gain=final_norm if last else None)
    return h
```
